```python
import math, functools
import jax, jax.numpy as jnp
from jax import lax
import numpy as np

D_MODEL = 1024
BATCH = 2
SEQ = 8192
DEPTH = 2
DEC_BATCH = 128
DEC_SEQ = 4
PAST_LEN = 8192
PAGE_SIZE = 128

MLA_HEADS = 6
MLA_NOPE = 64
MLA_ROPE = 32
MLA_V = 64
MLA_Q_RANK = D_MODEL // 4
MLA_KV_RANK = 4 * MLA_V
SB_HEADS = 6
SB_DIM = 64
DF_HEADS = 4
DF_KV_HEADS = 2
DF_D = 32
DF_V = 2 * DF_D
MIX_WIDTH = MLA_HEADS * MLA_V + SB_HEADS * SB_DIM + DF_HEADS * DF_V
D_FF = ((8 * D_MODEL // 3 + 127) // 128) * 128
ROPE_THETA = 10000.0
Q_BLOCK = 128
EPS = 1e-6
PROJ_WIDTHS = (MLA_Q_RANK, MLA_KV_RANK, MLA_ROPE,
               SB_HEADS * SB_DIM, SB_HEADS * SB_DIM, SB_HEADS * SB_DIM,
               DF_HEADS * 2 * DF_D, DF_KV_HEADS * 2 * DF_D, DF_KV_HEADS * DF_V)
PROJ_SPLITS = tuple(int(s) for s in np.cumsum(PROJ_WIDTHS)[:-1])
PROJ_WIDTH = sum(PROJ_WIDTHS)

kernel_name = 'hybrid_mla_stickbreak_diffattn_step'


def rms_norm(x, g):
    xf = x.astype(jnp.float32)
    y = xf * lax.rsqrt(jnp.mean(xf * xf, axis=-1, keepdims=True) + EPS)
    return (y * g.astype(jnp.float32)).astype(x.dtype)


def swiglu(h, w_gate, w_up, w_down):
    return (jax.nn.silu(h @ w_gate) * (h @ w_up)) @ w_down


def rope(x, pos):
    d = x.shape[-1]
    inv_freq = ROPE_THETA ** (-jnp.arange(0, d, 2, dtype=jnp.float32) / d)
    ang = pos.astype(jnp.float32)[:, None] * inv_freq
    ang = ang.reshape(ang.shape[0], *([1] * (x.ndim - 3)), d // 2)
    cos, sin = jnp.cos(ang), jnp.sin(ang)
    x1 = x[..., : d // 2].astype(jnp.float32)
    x2 = x[..., d // 2:].astype(jnp.float32)
    return jnp.concatenate([x1 * cos - x2 * sin, x2 * cos + x1 * sin], axis=-1).astype(x.dtype)


def sweep(fn, qs, q_pos):
    b, t = qs[0].shape[:2]
    if t <= Q_BLOCK:
        return fn(qs, q_pos)
    nb = t // Q_BLOCK
    to_blocks = lambda a: jnp.moveaxis(a.reshape(b, nb, Q_BLOCK, *a.shape[2:]), 1, 0)
    xs = (tuple(to_blocks(q) for q in qs), q_pos.reshape(nb, Q_BLOCK))
    out = lax.map(lambda blk: fn(blk[0], blk[1]), xs)
    return jnp.moveaxis(out, 0, 1).reshape(b, t, *out.shape[3:])


def softmax_attn(qs, q_pos, ks, v, k_pos, scale):
    b, tq, h = qs[0].shape[:3]
    tk = v.shape[1]
    s = sum(jnp.einsum('bqgrd,bkgd->bgrqk',
                       q.reshape(b, tq, k.shape[2], h // k.shape[2], q.shape[-1]), k,
                       preferred_element_type=jnp.float32).reshape(b, h, tq, tk)
            for q, k in zip(qs, ks))
    s = jnp.where(k_pos[None, :] <= q_pos[:, None], s * scale, -jnp.inf)
    p = jax.nn.softmax(s, axis=-1)
    gv = v.shape[2]
    o = jnp.einsum('bgrqk,bkgd->bqgrd', p.reshape(b, gv, h // gv, tq, tk).astype(v.dtype), v)
    return o.reshape(b, tq, h, v.shape[-1])


def stick_breaking(qs, q_pos, k, v, k_pos):
    q = qs[0]
    z = jnp.einsum('bqhd,bkhd->bhqk', q, k, preferred_element_type=jnp.float32) * (q.shape[-1] ** -0.5)
    before = k_pos[None, :] < q_pos[:, None]
    log_keep = jnp.where(before, jax.nn.log_sigmoid(-z), 0.0)
    later = lax.cumsum(log_keep, axis=3, reverse=True) - log_keep
    a = jnp.where(before, jnp.exp(jax.nn.log_sigmoid(z) + later), 0.0)
    return jnp.einsum('bhqk,bkhd->bqhd', a.astype(v.dtype), v)


def gather_pages(cache_l, page_table):
    g = cache_l[page_table]
    return g.reshape(page_table.shape[0], -1, *cache_l.shape[2:])


def token_mix(h, pos, past, p, layer_idx):
    b, t, _ = h.shape
    (c_q, c_kv, k_r, q_sb, k_sb, v_sb, q_df, k_df, v_df) = jnp.split(h @ p['w_in'], PROJ_SPLITS, axis=-1)
    q = (rms_norm(c_q, p['g_q_lat']) @ p['w_uq']).reshape(b, t, MLA_HEADS, MLA_NOPE + MLA_ROPE)
    q_nope, q_rope = q[..., :MLA_NOPE], rope(q[..., MLA_NOPE:], pos)
    ckv = rms_norm(c_kv, p['g_kv_lat'])
    krope = rope(k_r, pos)
    q_sb = q_sb.reshape(b, t, SB_HEADS, SB_DIM)
    k_sb = k_sb.reshape(b, t, SB_HEADS, SB_DIM)
    v_sb = v_sb.reshape(b, t, SB_HEADS, SB_DIM)
    q_df = rope(q_df.reshape(b, t, DF_HEADS, 2, DF_D), pos)
    k_df = rope(k_df.reshape(b, t, DF_KV_HEADS, 2, DF_D), pos)
    v_df = v_df.reshape(b, t, DF_KV_HEADS, DF_V)
    new_rows = (ckv, krope, k_sb, v_sb, k_df, v_df)
    if past is None:
        ckv_a, krope_a, ksb_a, vsb_a, kdf_a, vdf_a = new_rows
        k_pos = pos
    else:
        ckv_a, krope_a, ksb_a, vsb_a, kdf_a, vdf_a = tuple(
            jnp.concatenate([a, n], axis=1) for a, n in zip(past, new_rows))
        k_pos = jnp.arange(past[0].shape[1] + t, dtype=jnp.int32)

    mla_scale = (MLA_NOPE + MLA_ROPE) ** -0.5
    kr = krope_a[:, :, None, :]
    if past is None:
        k_nope = jnp.einsum('bkr,rhn->bkhn', ckv_a, p['w_uk'])
        v_mla = jnp.einsum('bkr,rhv->bkhv', ckv_a, p['w_uv'])
        fn = functools.partial(softmax_attn, ks=(k_nope, kr), v=v_mla, k_pos=k_pos, scale=mla_scale)
        o_mla = sweep(fn, (q_nope, q_rope), pos)
    else:
        q_lat = jnp.einsum('bqhn,rhn->bqhr', q_nope, p['w_uk'])
        lat = ckv_a[:, :, None, :]
        fn = functools.partial(softmax_attn, ks=(lat, kr), v=lat, k_pos=k_pos, scale=mla_scale)
        o_mla = jnp.einsum('bqhr,rhv->bqhv', sweep(fn, (q_lat, q_rope), pos), p['w_uv'])

    o_sb = sweep(functools.partial(stick_breaking, k=ksb_a, v=vsb_a, k_pos=k_pos), (q_sb,), pos)

    lam_init = 0.8 - 0.6 * math.exp(-0.3 * layer_idx)
    f32 = jnp.float32
    lam = (jnp.exp(jnp.sum(p['lam_q1'].astype(f32) * p['lam_k1'].astype(f32)))
           - jnp.exp(jnp.sum(p['lam_q2'].astype(f32) * p['lam_k2'].astype(f32))) + lam_init)
    df = functools.partial(softmax_attn, v=vdf_a, k_pos=k_pos, scale=DF_D ** -0.5)
    o1 = sweep(functools.partial(df, ks=(kdf_a[..., 0, :],)), (q_df[..., 0, :],), pos)
    o2 = sweep(functools.partial(df, ks=(kdf_a[..., 1, :],)), (q_df[..., 1, :],), pos)
    o_df = rms_norm(o1 - lam.astype(o1.dtype) * o2, p['g_df_sub']) * (1.0 - lam_init)

    mixed = jnp.concatenate([
        rms_norm(o_mla.reshape(b, t, -1), p['g_mla_out']),
        rms_norm(o_sb.reshape(b, t, -1), p['g_sb_out']),
        o_df.reshape(b, t, -1)], axis=-1)
    return mixed @ p['w_out'], new_rows


def decoder_layer(x, pos, past, p, layer_idx):
    x = x + 0.5 * swiglu(rms_norm(x, p['g_ffn1']), p['w1_gate'], p['w1_up'], p['w1_down'])
    mixed, new_rows = token_mix(rms_norm(x, p['g_mix']), pos, past, p, layer_idx)
    x = x + mixed
    x = x + 0.5 * swiglu(rms_norm(x, p['g_ffn2']), p['w2_gate'], p['w2_up'], p['w2_down'])
    return x, new_rows


def stack_rows(rows, i):
    return jnp.stack([r[i] for r in rows], axis=0)


def setup_inputs(seed: int = 0) -> dict:
    key = jax.random.key(seed)
    keys = iter(jax.random.split(key, 48))
    n_pages = PAST_LEN // PAGE_SIZE
    n_used = DEC_BATCH * n_pages
    n_pool = n_used + n_used // 4
    L = DEPTH

    def normal(shape, scale=1.0):
        return scale * jax.random.normal(next(keys), shape, jnp.float32)

    def weight(shape, fan_in):
        return normal(shape, fan_in ** -0.5)

    def gain(shape):
        return 1.0 + 0.02 * normal(shape)

    return {
        'x_prompt': normal((BATCH, SEQ, D_MODEL)),
        'x_sample': normal((DEC_BATCH, DEC_SEQ, D_MODEL)),
        'cache_mla_ckv': normal((L, n_pool, PAGE_SIZE, MLA_KV_RANK)),
        'cache_mla_krope': normal((L, n_pool, PAGE_SIZE, MLA_ROPE)),
        'cache_sb_k': normal((L, n_pool, PAGE_SIZE, SB_HEADS, SB_DIM)),
        'cache_sb_v': normal((L, n_pool, PAGE_SIZE, SB_HEADS, SB_DIM)),
        'cache_df_k': normal((L, n_pool, PAGE_SIZE, DF_KV_HEADS, 2, DF_D)),
        'cache_df_v': normal((L, n_pool, PAGE_SIZE, DF_KV_HEADS, DF_V)),
        'page_table': jax.random.permutation(next(keys), n_pool)[:n_used].reshape(DEC_BATCH, n_pages).astype(jnp.int32),
        'g_ffn1': gain((L, D_MODEL)),
        'w1_gate': weight((L, D_MODEL, D_FF), D_MODEL),
        'w1_up': weight((L, D_MODEL, D_FF), D_MODEL),
        'w1_down': weight((L, D_FF, D_MODEL), D_FF),
        'g_mix': gain((L, D_MODEL)),
        'w_in': weight((L, D_MODEL, PROJ_WIDTH), D_MODEL),
        'g_q_lat': gain((L, MLA_Q_RANK)),
        'w_uq': weight((L, MLA_Q_RANK, MLA_HEADS * (MLA_NOPE + MLA_ROPE)), MLA_Q_RANK),
        'g_kv_lat': gain((L, MLA_KV_RANK)),
        'w_uk': weight((L, MLA_KV_RANK, MLA_HEADS, MLA_NOPE), MLA_KV_RANK),
        'w_uv': weight((L, MLA_KV_RANK, MLA_HEADS, MLA_V), MLA_KV_RANK),
        'g_mla_out': gain((L, MLA_HEADS * MLA_V)),
        'g_sb_out': gain((L, SB_HEADS * SB_DIM)),
        'lam_q1': normal((L, DF_D), 0.1),
        'lam_k1': normal((L, DF_D), 0.1),
        'lam_q2': normal((L, DF_D), 0.1),
        'lam_k2': normal((L, DF_D), 0.1),
        'g_df_sub': gain((L, DF_V)),
        'w_out': weight((L, MIX_WIDTH, D_MODEL), MIX_WIDTH),
        'g_ffn2': gain((L, D_MODEL)),
        'w2_gate': weight((L, D_MODEL, D_FF), D_MODEL),
        'w2_up': weight((L, D_MODEL, D_FF), D_MODEL),
        'w2_down': weight((L, D_FF, D_MODEL), D_FF),
        'g_final': gain((D_MODEL,)),
    }


def reference(x_prompt, x_sample, cache_mla_ckv, cache_mla_krope, cache_sb_k, cache_sb_v,
              cache_df_k, cache_df_v, page_table, g_ffn1, w1_gate, w1_up, w1_down, g_mix, w_in,
              g_q_lat, w_uq, g_kv_lat, w_uk, w_uv, g_mla_out, g_sb_out, lam_q1, lam_k1, lam_q2,
              lam_k2, g_df_sub, w_out, g_ffn2, w2_gate, w2_up, w2_down, g_final):
    past_len = page_table.shape[1] * PAGE_SIZE
    pos_p = jnp.arange(x_prompt.shape[1], dtype=jnp.int32)
    pos_s = past_len + jnp.arange(x_sample.shape[1], dtype=jnp.int32)
    caches = (cache_mla_ckv, cache_mla_krope, cache_sb_k, cache_sb_v, cache_df_k, cache_df_v)
    hp, hs = x_prompt, x_sample
    rows_p, rows_s = [], []
    for l in range(DEPTH):
        prm = dict(g_ffn1=g_ffn1[l], w1_gate=w1_gate[l], w1_up=w1_up[l], w1_down=w1_down[l],
                   g_mix=g_mix[l], w_in=w_in[l], g_q_lat=g_q_lat[l], w_uq=w_uq[l],
                   g_kv_lat=g_kv_lat[l], w_uk=w_uk[l], w_uv=w_uv[l], g_mla_out=g_mla_out[l],
                   g_sb_out=g_sb_out[l], lam_q1=lam_q1[l], lam_k1=lam_k1[l], lam_q2=lam_q2[l],
                   lam_k2=lam_k2[l], g_df_sub=g_df_sub[l], w_out=w_out[l], g_ffn2=g_ffn2[l],
                   w2_gate=w2_gate[l], w2_up=w2_up[l], w2_down=w2_down[l])
        hp, new_p = decoder_layer(hp, pos_p, None, prm, l)
        past = tuple(gather_pages(c[l], page_table) for c in caches)
        hs, new_s = decoder_layer(hs, pos_s, past, prm, l)
        rows_p.append(new_p)
        rows_s.append(new_s)
    y_prompt = rms_norm(hp, g_final)
    y_sample = rms_norm(hs, g_final)
    return (y_prompt, y_sample,
            stack_rows(rows_p, 0), stack_rows(rows_s, 0),
            stack_rows(rows_p, 1), stack_rows(rows_s, 1),
            stack_rows(rows_p, 2), stack_rows(rows_s, 2),
            stack_rows(rows_p, 3), stack_rows(rows_s, 3),
            stack_rows(rows_p, 4), stack_rows(rows_s, 4),
            stack_rows(rows_p, 5), stack_rows(rows_s, 5))
```

```python
import functools
import math

import jax
import jax.numpy as jnp
from jax import lax
from jax.experimental import pallas as pl
from jax.experimental.pallas import tpu as pltpu

F32 = jnp.float32
BF16 = jnp.bfloat16

EPS = 1e-6
ROPE_THETA = 10000.0
MLA_HEADS, MLA_NOPE, MLA_ROPE, MLA_V = 6, 64, 32, 64
SB_HEADS, SB_DIM = 6, 64
DF_HEADS, DF_KV_HEADS, DF_D, DF_V = 4, 2, 32, 64
MLA_SCALE = (MLA_NOPE + MLA_ROPE) ** -0.5
SB_SCALE = SB_DIM ** -0.5
DF_SCALE = DF_D ** -0.5

LANES = 128
DEC_ROWS = 16
VMEM_LIMIT = 56 * 1024 * 1024
NEG_BIG = -1e30


def _cparams(sem):
    return pltpu.CompilerParams(dimension_semantics=sem, vmem_limit_bytes=VMEM_LIMIT)


def _rms(x, g):
    return x * lax.rsqrt(jnp.mean(x * x, axis=-1, keepdims=True) + EPS) * g


def _dot(a, b):
    return jnp.dot(a, b, preferred_element_type=F32)


def _dot_t(a, b):
    return lax.dot_general(a, b, (((1,), (1,)), ((), ())), preferred_element_type=F32)


def _pick_tile(n, candidates):
    for c in candidates:
        if n % c == 0:
            return c
    raise ValueError(f"no tile for {n}")


def _full(shape):
    nd = len(shape)
    return pl.BlockSpec(shape, lambda *_: (0,) * nd)


def _ffn_body(x_ref, g_ref, wg_ref, wu_ref, wd_ref, gf_ref, o_ref, *, ff_chunk, final_norm):
    x = x_ref[...]
    h = _rms(x, g_ref[...]).astype(BF16)
    acc = jnp.zeros_like(x)
    d_ff = wg_ref.shape[1]
    for c0 in range(0, d_ff, ff_chunk):
        gate = _dot(h, wg_ref[:, c0:c0 + ff_chunk])
        up = _dot(h, wu_ref[:, c0:c0 + ff_chunk])
        act = (gate * jax.nn.sigmoid(gate) * up).astype(BF16)
        acc = acc + _dot(act, wd_ref[c0:c0 + ff_chunk, :])
    y = x + 0.5 * acc
    if final_norm:
        y = _rms(y, gf_ref[...])
    o_ref[...] = y


def _ffn(x, g, wg, wu, wd, g_final, final_norm):
    n, d = x.shape
    d_ff = wg.shape[1]
    tm = _pick_tile(n, (512, 256, 128, 64, 32, 16, 8))
    ff_chunk = d_ff // 2 if (d_ff // 2) % LANES == 0 else d_ff
    row = pl.BlockSpec((tm, d), lambda i: (i, 0))
    return pl.pallas_call(
        functools.partial(_ffn_body, ff_chunk=ff_chunk, final_norm=final_norm),
        grid=(n // tm,),
        in_specs=[row, _full((1, d)), _full((d, d_ff)), _full((d, d_ff)), _full((d_ff, d)), _full((1, d))],
        out_specs=row,
        out_shape=jax.ShapeDtypeStruct((n, d), F32),
        compiler_params=_cparams(("parallel",)),
        name="ffn",
    )(x, g, wg, wu, wd, g_final)


def _rope128(x, cos, sin):
    lane = lax.broadcasted_iota(jnp.int32, x.shape, 1)
    first = (lane % 32) < 16
    rot = jnp.where(first, -pltpu.roll(x, LANES - 16, 1), pltpu.roll(x, 16, 1))
    return x * cos + rot * sin


_CQ, _CKV, _KR, _QSB, _KSB, _VSB, _QDF, _KDF, _VDF, _PROJ_W = 0, 256, 512, 640, 1024, 1408, 1792, 2048, 2176, 2304


def _inproj_body(x_ref, g_ref, win_ref, gq_ref, wuq_ref, gkv_ref, wuk_ref, wuv_ref, cos_ref, sin_ref,
                 ckv_o, krope_o, ksb_o, vsb_o, kdf_o, vdf_o,
                 qm_o, km_o, vm_o, qsb_o, ksbb_o, vsbb_o, qdf_o, kdfb_o, vdfb_o):
    h = _rms(x_ref[...], g_ref[...]).astype(BF16)
    proj = _dot(h, win_ref[...])
    cos = cos_ref[...]
    sin = sin_ref[...]

    cq = _rms(proj[:, _CQ:_CQ + 256], gq_ref[...]).astype(BF16)
    q = _dot(cq, wuq_ref[...])
    for hd in range(MLA_HEADS):
        base = hd * 256
        qm_o[:, base:base + LANES] = (q[:, base:base + LANES] * MLA_SCALE).astype(BF16)
        qr = _rope128(q[:, base + LANES:base + 2 * LANES], cos, sin)
        qm_o[:, base + LANES:base + 2 * LANES] = (qr * MLA_SCALE).astype(BF16)
    ckv = _rms(proj[:, _CKV:_CKV + 256], gkv_ref[...])
    ckv_o[...] = ckv
    ckv_b = ckv.astype(BF16)
    k_nope = _dot(ckv_b, wuk_ref[...])
    vm_o[...] = _dot(ckv_b, wuv_ref[...]).astype(BF16)
    krope = _rope128(proj[:, _KR:_KR + LANES], cos, sin)
    krope_o[...] = krope[:, :MLA_ROPE]
    krope_b = krope.astype(BF16)
    for p in range(MLA_HEADS // 2):
        km_o[:, p * 256:p * 256 + LANES] = k_nope[:, p * LANES:(p + 1) * LANES].astype(BF16)
        km_o[:, p * 256 + LANES:(p + 1) * 256] = krope_b

    qsb_o[...] = (proj[:, _QSB:_QSB + 384] * SB_SCALE).astype(BF16)
    ksb = proj[:, _KSB:_KSB + 384]
    vsb = proj[:, _VSB:_VSB + 384]
    ksb_o[...] = ksb
    vsb_o[...] = vsb
    ksbb_o[...] = ksb.astype(BF16)
    vsbb_o[...] = vsb.astype(BF16)

    for c in range(2):
        qd = _rope128(proj[:, _QDF + c * LANES:_QDF + (c + 1) * LANES], cos, sin)
        qdf_o[:, c * LANES:(c + 1) * LANES] = (qd * DF_SCALE).astype(BF16)
    kdf = _rope128(proj[:, _KDF:_KDF + LANES], cos, sin)
    vdf = proj[:, _VDF:_VDF + LANES]
    kdf_o[...] = kdf
    vdf_o[...] = vdf
    kdfb_o[...] = kdf.astype(BF16)
    vdfb_o[...] = vdf.astype(BF16)


def _inproj(x, g_mix, w_in, g_q, w_uq, g_kv, w_uk, w_uv, cos, sin):
    n, d = x.shape
    tm = _pick_tile(n, (512, 256, 128, 64, 32, 16, 8))
    row = lambda w: pl.BlockSpec((tm, w), lambda i: (i, 0))
    widths_f32 = (256, MLA_ROPE, 384, 384, LANES, LANES)
    widths_bf16 = (MLA_HEADS * 256, 3 * 256, 384, 384, 384, 384, 256, LANES, LANES)
    out_shape = ([jax.ShapeDtypeStruct((n, w), F32) for w in widths_f32]
                 + [jax.ShapeDtypeStruct((n, w), BF16) for w in widths_bf16])
    return pl.pallas_call(
        _inproj_body,
        grid=(n // tm,),
        in_specs=[row(d), _full((1, d)), _full(w_in.shape), _full((1, 256)), _full(w_uq.shape), _full((1, 256)),
                  _full(w_uk.shape), _full(w_uv.shape), row(LANES), row(LANES)],
        out_specs=[row(w) for w in widths_f32 + widths_bf16],
        out_shape=out_shape,
        compiler_params=_cparams(("parallel",)),
        name="inproj",
    )(x, g_mix, w_in, g_q, w_uq, g_kv, w_uk, w_uv, cos, sin)


def _outproj_body(x_ref, om_ref, os_ref, od_ref, gm_ref, gs_ref, w_ref, o_ref):
    mixed_m = _rms(om_ref[...], gm_ref[...]).astype(BF16)
    mixed_s = _rms(os_ref[...], gs_ref[...]).astype(BF16)
    mixed_d = od_ref[...].astype(BF16)
    y = _dot(mixed_m, w_ref[0:384, :]) + _dot(mixed_s, w_ref[384:768, :]) + _dot(mixed_d, w_ref[768:1024, :])
    o_ref[...] = x_ref[...] + y


def _outproj(x, o_mla, o_sb, o_df, g_mla, g_sb, w_out):
    n, d = x.shape
    tm = _pick_tile(n, (512, 256, 128, 64, 32, 16, 8))
    row = lambda w: pl.BlockSpec((tm, w), lambda i: (i, 0))
    return pl.pallas_call(
        _outproj_body,
        grid=(n // tm,),
        in_specs=[row(d), row(384), row(384), row(256), _full((1, 384)), _full((1, 384)), _full(w_out.shape)],
        out_specs=row(d),
        out_shape=jax.ShapeDtypeStruct((n, d), F32),
        compiler_params=_cparams(("parallel",)),
        name="outproj",
    )(x, o_mla, o_sb, o_df, g_mla, g_sb, w_out)


def _softmax_attend(q, k_ref, v_ref, qi, tq):
    def step(k, v, carry, mask):
        m, l, acc = carry
        s = _dot_t(q, k)
        if mask is not None:
            s = jnp.where(mask, s, -jnp.inf)
        m_new = jnp.maximum(m, jnp.max(s, axis=-1, keepdims=True))
        p = jnp.exp(s - m_new)
        alpha = jnp.exp(m - m_new)
        l = alpha * l + jnp.sum(p, axis=-1, keepdims=True)
        acc = alpha * acc + _dot(p.astype(BF16), v)
        return m_new, l, acc

    def body(j, carry):
        off = pl.multiple_of(j * tq, tq)
        return step(k_ref[pl.ds(off, tq), :], v_ref[pl.ds(off, tq), :], carry, None)

    init = (jnp.full((tq, 1), NEG_BIG, F32), jnp.zeros((tq, 1), F32), jnp.zeros((tq, LANES), F32))
    carry = lax.fori_loop(0, qi, body, init)
    off = pl.multiple_of(qi * tq, tq)
    row = lax.broadcasted_iota(jnp.int32, (tq, tq), 0)
    col = lax.broadcasted_iota(jnp.int32, (tq, tq), 1)
    _, l, acc = step(k_ref[pl.ds(off, tq), :], v_ref[pl.ds(off, tq), :], carry, col <= row)
    return acc, l


def _mla_prompt_body(q_ref, k_ref, v_ref, o_ref, *, tq):
    qi = pl.program_id(2)
    lane = lax.broadcasted_iota(jnp.int32, (tq, LANES), 1)
    out = jnp.zeros((tq, LANES), F32)
    for hh in range(2):
        acc, l = _softmax_attend(q_ref[:, hh * 256:(hh + 1) * 256], k_ref, v_ref, qi, tq)
        out = jnp.where((lane // 64) == hh, acc / l, out)
    o_ref[...] = out


def _mla_prompt(qm, km, vm, b, t, tq):
    nq = t // tq
    return pl.pallas_call(
        functools.partial(_mla_prompt_body, tq=tq),
        grid=(b, MLA_HEADS // 2, nq),
        in_specs=[pl.BlockSpec((tq, 512), lambda bi, p, qi: (bi * nq + qi, p)),
                  pl.BlockSpec((t, 256), lambda bi, p, qi: (bi, p)),
                  pl.BlockSpec((t, LANES), lambda bi, p, qi: (bi, p))],
        out_specs=pl.BlockSpec((tq, LANES), lambda bi, p, qi: (bi * nq + qi, p)),
        out_shape=jax.ShapeDtypeStruct((b * t, 384), F32),
        compiler_params=_cparams(("parallel", "parallel", "arbitrary")),
        name="mla_prompt",
    )(qm, km, vm)


def _log_sigmoid_pair(z):
    t = jnp.log1p(jnp.exp(-jnp.abs(z)))
    return jnp.minimum(-z, 0.0) - t, jnp.minimum(z, 0.0) - t


def _suffix_sum(lk, u):
    hi = lk.astype(BF16)
    lo = (lk - hi.astype(F32)).astype(BF16)
    return _dot(hi, u) + _dot(lo, u)


def _sb_attend(q, k_ref, v_ref, u, qi, tq):
    def step(k, v, carry, mask):
        c, acc = carry
        z = _dot_t(q, k)
        lk, ls = _log_sigmoid_pair(z)
        if mask is not None:
            lk = jnp.where(mask, lk, 0.0)
        later = _suffix_sum(lk, u) + c
        a = jnp.exp(ls + later)
        if mask is not None:
            a = jnp.where(mask, a, 0.0)
        acc = acc + _dot(a.astype(BF16), v)
        c = c + jnp.sum(lk, axis=-1, keepdims=True)
        return c, acc

    off = pl.multiple_of(qi * tq, tq)
    row = lax.broadcasted_iota(jnp.int32, (tq, tq), 0)
    col = lax.broadcasted_iota(jnp.int32, (tq, tq), 1)
    init = (jnp.zeros((tq, 1), F32), jnp.zeros((tq, LANES), F32))
    carry = step(k_ref[pl.ds(off, tq), :], v_ref[pl.ds(off, tq), :], init, col < row)

    def body(i, carry):
        o = pl.multiple_of((qi - 1 - i) * tq, tq)
        return step(k_ref[pl.ds(o, tq), :], v_ref[pl.ds(o, tq), :], carry, None)

    _, acc = lax.fori_loop(0, qi, body, carry)
    return acc


def _sb_prompt_body(q_ref, k_ref, v_ref, u_ref, o_ref, *, tq):
    qi = pl.program_id(2)
    lane = lax.broadcasted_iota(jnp.int32, (tq, LANES), 1)
    q = q_ref[...].astype(F32)
    u = u_ref[...]
    out = jnp.zeros((tq, LANES), F32)
    for hh in range(2):
        own = (lane // 64) == hh
        acc = _sb_attend(jnp.where(own, q, 0.0).astype(BF16), k_ref, v_ref, u, qi, tq)
        out = jnp.where(own, acc, out)
    o_ref[...] = out


def _sb_prompt(qsb, ksb, vsb, u, b, t, tq):
    nq = t // tq
    return pl.pallas_call(
        functools.partial(_sb_prompt_body, tq=tq),
        grid=(b, SB_HEADS // 2, nq),
        in_specs=[pl.BlockSpec((tq, LANES), lambda bi, p, qi: (bi * nq + qi, p)),
                  pl.BlockSpec((t, LANES), lambda bi, p, qi: (bi, p)),
                  pl.BlockSpec((t, LANES), lambda bi, p, qi: (bi, p)),
                  _full((tq, tq))],
        out_specs=pl.BlockSpec((tq, LANES), lambda bi, p, qi: (bi * nq + qi, p)),
        out_shape=jax.ShapeDtypeStruct((b * t, 384), F32),
        compiler_params=_cparams(("parallel", "parallel", "arbitrary")),
        name="sb_prompt",
    )(qsb, ksb, vsb, u)


def _lam(lq1, lk1, lq2, lk2, lam_init):
    return (jnp.exp(jnp.sum(lq1 * lk1, axis=-1, keepdims=True))
            - jnp.exp(jnp.sum(lq2 * lk2, axis=-1, keepdims=True)) + lam_init)


def _df_prompt_body(q_ref, k_ref, v_ref, lq1_ref, lk1_ref, lq2_ref, lk2_ref, g_ref, o_ref, *, tq, lam_init):
    qi = pl.program_id(2)
    lane = lax.broadcasted_iota(jnp.int32, (tq, LANES), 1)
    q = q_ref[...].astype(F32)
    lam = _lam(lq1_ref[...], lk1_ref[...], lq2_ref[...], lk2_ref[...], lam_init)
    out = jnp.zeros((tq, LANES), F32)
    for g in range(DF_KV_HEADS):
        o = []
        for c in range(2):
            sel = (lane // 32) == (2 * g + c)
            acc, l = _softmax_attend(jnp.where(sel, q, 0.0).astype(BF16), k_ref, v_ref, qi, tq)
            o.append(acc / l)
        own = (lane // 64) == g
        d = jnp.where(own, o[0] - lam * o[1], 0.0)
        ms = jnp.sum(d * d, axis=-1, keepdims=True) * (1.0 / DF_V)
        y = d * lax.rsqrt(ms + EPS) * g_ref[...] * (1.0 - lam_init)
        out = jnp.where(own, y, out)
    o_ref[...] = out


def _df_prompt(qdf, kdf, vdf, lams, g_df2, b, t, tq, lam_init):
    nq = t // tq
    return pl.pallas_call(
        functools.partial(_df_prompt_body, tq=tq, lam_init=lam_init),
        grid=(b, 2, nq),
        in_specs=[pl.BlockSpec((tq, LANES), lambda bi, r, qi: (bi * nq + qi, r)),
                  pl.BlockSpec((t, LANES), lambda bi, r, qi: (bi, 0)),
                  pl.BlockSpec((t, LANES), lambda bi, r, qi: (bi, 0)),
                  _full((1, DF_D)), _full((1, DF_D)), _full((1, DF_D)), _full((1, DF_D)), _full((1, LANES))],
        out_specs=pl.BlockSpec((tq, LANES), lambda bi, r, qi: (bi * nq + qi, r)),
        out_shape=jax.ShapeDtypeStruct((b * t, 256), F32),
        compiler_params=_cparams(("parallel", "parallel", "arbitrary")),
        name="df_prompt",
    )(qdf, kdf, vdf, *lams, g_df2)


def _page_specs(block, layer, npg, reverse_chunks, n_chunks):
    nz = len(block) - 2

    def imap(bi, c, pt, *, slot):
        chunk = (n_chunks - 1 - c) if reverse_chunks else c
        return (layer, pt[bi, chunk * npg + slot]) + (0,) * nz

    return [pl.BlockSpec(block, functools.partial(imap, slot=s)) for s in range(npg)]


def _online_update(s, m_ref, l_ref, acc_ref, pv):
    m = m_ref[...]
    m_new = jnp.maximum(m, jnp.max(s, axis=-1, keepdims=True))
    p = jnp.exp(s - m_new)
    alpha = jnp.exp(m - m_new)
    l_ref[...] = alpha * l_ref[...] + jnp.sum(p, axis=-1, keepdims=True)
    acc_ref[...] = alpha * acc_ref[...] + pv(p.astype(BF16))
    m_ref[...] = m_new


def _new_key_mask(rows, n_new, strict):
    t = lax.broadcasted_iota(jnp.int32, (rows, DEC_ROWS), 0) % DEC_ROWS
    kk = lax.broadcasted_iota(jnp.int32, (rows, DEC_ROWS), 1)
    return ((kk < t) if strict else (kk <= t)) & (kk < n_new)


def _mla_dec_body(pt_ref, q_ref, cnew_ref, rnew_ref, wukt_ref, wuv_ref, *rest, npg, n_chunks, n_new):
    ckv_pages = rest[:npg]
    kr_pages = rest[npg:2 * npg]
    o_ref = rest[2 * npg]
    kvb, krb, qlat, qrope, m_ref, l_ref, acc_ref = rest[2 * npg + 1:]
    c = pl.program_id(1)
    rows = MLA_HEADS * DEC_ROWS

    @pl.when(c == 0)
    def _():
        for hd in range(MLA_HEADS):
            base = hd * 256
            ql = _dot(q_ref[:, base:base + LANES], wukt_ref[hd // 2])
            qlat[hd * DEC_ROWS:(hd + 1) * DEC_ROWS, :] = ql.astype(BF16)
            qrope[hd * DEC_ROWS:(hd + 1) * DEC_ROWS, :] = q_ref[:, base + LANES:base + 2 * LANES]
        m_ref[...] = jnp.full(m_ref.shape, NEG_BIG, F32)
        l_ref[...] = jnp.zeros(l_ref.shape, F32)
        acc_ref[...] = jnp.zeros(acc_ref.shape, F32)

    for i in range(npg):
        kvb[i * LANES:(i + 1) * LANES, :] = ckv_pages[i][...].astype(BF16)
        krb[:, i * LANES:(i + 1) * LANES] = kr_pages[i][...].astype(BF16)
    ql = qlat[...]
    qr = qrope[:, :MLA_ROPE]
    s = _dot_t(ql, kvb[...]) + _dot(qr, krb[...])
    _online_update(s, m_ref, l_ref, acc_ref, lambda p: _dot(p, kvb[...]))

    @pl.when(c == n_chunks - 1)
    def _():
        cn = cnew_ref[...].astype(BF16)
        rn = rnew_ref[...].astype(BF16)
        s_new = _dot_t(ql, cn) + _dot_t(qr, rn)
        s_new = jnp.where(_new_key_mask(rows, n_new, False), s_new, -jnp.inf)
        _online_update(s_new, m_ref, l_ref, acc_ref, lambda p: _dot(p, cn))
        o_lat = (acc_ref[...] / l_ref[...]).astype(BF16)
        for hd in range(MLA_HEADS):
            o_ref[hd] = _dot(o_lat[hd * DEC_ROWS:(hd + 1) * DEC_ROWS, :], wuv_ref[:, hd * MLA_V:(hd + 1) * MLA_V])


def _mla_dec(page_table, qm_s, ckv_new, kr_new, w_ukt, w_uv, cache_ckv, cache_kr_t, layer, npg, n_new):
    db, n_pages = page_table.shape
    n_chunks = n_pages // npg
    rows = MLA_HEADS * DEC_ROWS
    seq = lambda w: pl.BlockSpec((None, DEC_ROWS, w), lambda bi, c, pt: (bi, 0, 0))
    const = lambda shape: pl.BlockSpec(shape, lambda bi, c, pt: (0,) * len(shape))
    in_specs = ([seq(MLA_HEADS * 256), seq(256), seq(MLA_ROPE), const(w_ukt.shape), const(w_uv.shape)]
                + _page_specs((None, None, LANES, 256), layer, npg, False, n_chunks)
                + _page_specs((None, None, MLA_ROPE, LANES), layer, npg, False, n_chunks))
    gs = pltpu.PrefetchScalarGridSpec(
        num_scalar_prefetch=1, grid=(db, n_chunks), in_specs=in_specs,
        out_specs=pl.BlockSpec((None, MLA_HEADS, DEC_ROWS, MLA_V), lambda bi, c, pt: (bi, 0, 0, 0)),
        scratch_shapes=[pltpu.VMEM((npg * LANES, 256), BF16), pltpu.VMEM((MLA_ROPE, npg * LANES), BF16),
                        pltpu.VMEM((rows, 256), BF16), pltpu.VMEM((rows, LANES), BF16),
                        pltpu.VMEM((rows, 1), F32), pltpu.VMEM((rows, 1), F32), pltpu.VMEM((rows, 256), F32)])
    return pl.pallas_call(
        functools.partial(_mla_dec_body, npg=npg, n_chunks=n_chunks, n_new=n_new),
        grid_spec=gs,
        out_shape=jax.ShapeDtypeStruct((db, MLA_HEADS, DEC_ROWS, MLA_V), F32),
        compiler_params=_cparams(("parallel", "arbitrary")),
        name="mla_dec",
    )(page_table, qm_s, ckv_new, kr_new, w_ukt, w_uv, *([cache_ckv] * npg), *([cache_kr_t] * npg))


def _sb_dec_body(pt_ref, q_ref, knew_ref, vnew_ref, u_ref, u8_ref, *rest, npg, n_chunks, n_new):
    k_pages = rest[:npg]
    v_pages = rest[npg:2 * npg]
    o_ref = rest[2 * npg]
    ktb, vtb, qs, c_ref, acc_ref = rest[2 * npg + 1:]
    c = pl.program_id(1)
    rows = SB_HEADS * DEC_ROWS

    def head_rows(hd):
        return slice(hd * DEC_ROWS, (hd + 1) * DEC_ROWS)

    @pl.when(c == 0)
    def _():
        q = q_ref[...].astype(F32)
        kn = knew_ref[...]
        vn = vnew_ref[...]
        zs = []
        for hd in range(SB_HEADS):
            qh = q[:, hd * SB_DIM:(hd + 1) * SB_DIM].astype(BF16)
            qs[head_rows(hd), :] = qh
            zs.append(_dot_t(qh, kn[:, hd * SB_DIM:(hd + 1) * SB_DIM].astype(BF16)))
        z = jnp.concatenate(zs, axis=0)
        mask = _new_key_mask(rows, n_new, True)
        lk, ls = _log_sigmoid_pair(z)
        lk = jnp.where(mask, lk, 0.0)
        a = jnp.where(mask, jnp.exp(ls + _suffix_sum(lk, u8_ref[...])), 0.0).astype(BF16)
        for hd in range(SB_HEADS):
            acc_ref[head_rows(hd), :] = _dot(a[head_rows(hd), :], vn[:, hd * SB_DIM:(hd + 1) * SB_DIM].astype(BF16))
        c_ref[...] = jnp.sum(lk, axis=-1, keepdims=True)

    for i in range(npg):
        for hd in range(SB_HEADS):
            ktb[hd, :, i * LANES:(i + 1) * LANES] = k_pages[i][hd].astype(BF16)
            vtb[hd, :, i * LANES:(i + 1) * LANES] = v_pages[i][hd].astype(BF16)
    z = jnp.concatenate([_dot(qs[head_rows(hd), :], ktb[hd]) for hd in range(SB_HEADS)], axis=0)
    lk, ls = _log_sigmoid_pair(z)
    u = u_ref[...]
    carry = c_ref[...]
    later = [None] * npg
    for blk in range(npg - 1, -1, -1):
        lk_b = lk[:, blk * LANES:(blk + 1) * LANES]
        later[blk] = _suffix_sum(lk_b, u) + carry
        carry = carry + jnp.sum(lk_b, axis=-1, keepdims=True)
    c_ref[...] = carry
    a = jnp.exp(ls + jnp.concatenate(later, axis=1)).astype(BF16)
    for hd in range(SB_HEADS):
        acc_ref[head_rows(hd), :] += _dot_t(a[head_rows(hd), :], vtb[hd])

    @pl.when(c == n_chunks - 1)
    def _():
        for hd in range(SB_HEADS):
            o_ref[hd] = acc_ref[head_rows(hd), :]


def _sb_dec(page_table, qsb_s, k_new, v_new, u, u8, cache_k_t, cache_v_t, layer, npg, n_new):
    db, n_pages = page_table.shape
    n_chunks = n_pages // npg
    rows = SB_HEADS * DEC_ROWS
    seq = lambda w: pl.BlockSpec((None, DEC_ROWS, w), lambda bi, c, pt: (bi, 0, 0))
    const = lambda shape: pl.BlockSpec(shape, lambda bi, c, pt: (0,) * len(shape))
    page = (None, None, SB_HEADS, SB_DIM, LANES)
    in_specs = ([seq(384), seq(384), seq(384), const(u.shape), const(u8.shape)]
                + _page_specs(page, layer, npg, True, n_chunks) + _page_specs(page, layer, npg, True, n_chunks))
    gs = pltpu.PrefetchScalarGridSpec(
        num_scalar_prefetch=1, grid=(db, n_chunks), in_specs=in_specs,
        out_specs=pl.BlockSpec((None, SB_HEADS, DEC_ROWS, SB_DIM), lambda bi, c, pt: (bi, 0, 0, 0)),
        scratch_shapes=[pltpu.VMEM((SB_HEADS, SB_DIM, npg * LANES), BF16), pltpu.VMEM((SB_HEADS, SB_DIM, npg * LANES), BF16),
                        pltpu.VMEM((rows, SB_DIM), BF16), pltpu.VMEM((rows, 1), F32), pltpu.VMEM((rows, SB_DIM), F32)])
    return pl.pallas_call(
        functools.partial(_sb_dec_body, npg=npg, n_chunks=n_chunks, n_new=n_new),
        grid_spec=gs,
        out_shape=jax.ShapeDtypeStruct((db, SB_HEADS, DEC_ROWS, SB_DIM), F32),
        compiler_params=_cparams(("parallel", "arbitrary")),
        name="sb_dec",
    )(page_table, qsb_s, k_new, v_new, u, u8, *([cache_k_t] * npg), *([cache_v_t] * npg))


def _df_q_offset(hd, c):
    return (hd % 2) * LANES + (hd // 2) * 64 + c * DF_D


def _df_dec_body(pt_ref, q_ref, knew_ref, vnew_ref, lq1_ref, lk1_ref, lq2_ref, lk2_ref, g_ref, *rest,
                 npg, n_chunks, n_new, lam_init):
    k_pages = rest[:npg]
    v_pages = rest[npg:2 * npg]
    o_ref = rest[2 * npg]
    ktb, vtb, qs, m_ref, l_ref, acc_ref = rest[2 * npg + 1:]
    c = pl.program_id(1)
    grp = 2 * DEC_ROWS
    rows = 2 * DF_KV_HEADS * grp

    @pl.when(c == 0)
    def _():
        q = q_ref[...].astype(F32)
        pieces = []
        for g in range(DF_KV_HEADS):
            for cc in range(2):
                for h2 in range(2):
                    off = _df_q_offset(2 * g + h2, cc)
                    pieces.append(q[:, off:off + DF_D])
        qs[...] = jnp.concatenate(pieces, axis=0).astype(BF16)
        m_ref[...] = jnp.full(m_ref.shape, NEG_BIG, F32)
        l_ref[...] = jnp.zeros(l_ref.shape, F32)
        acc_ref[...] = jnp.zeros(acc_ref.shape, F32)

    for i in range(npg):
        for g in range(DF_KV_HEADS):
            vtb[g, :, i * LANES:(i + 1) * LANES] = v_pages[i][g].astype(BF16)
            for cc in range(2):
                ktb[2 * g + cc, :, i * LANES:(i + 1) * LANES] = k_pages[i][g, cc].astype(BF16)

    def gc_rows(gc):
        return slice(gc * grp, (gc + 1) * grp)

    s = jnp.concatenate([_dot(qs[gc_rows(gc), :], ktb[gc]) for gc in range(2 * DF_KV_HEADS)], axis=0)

    def pv_pages(p):
        return jnp.concatenate([_dot_t(p[g * 2 * grp:(g + 1) * 2 * grp, :], vtb[g]) for g in range(DF_KV_HEADS)], axis=0)

    _online_update(s, m_ref, l_ref, acc_ref, pv_pages)

    @pl.when(c == n_chunks - 1)
    def _():
        kn = knew_ref[...].astype(BF16)
        vn = vnew_ref[...].astype(BF16)
        s_new = jnp.concatenate([_dot_t(qs[gc_rows(gc), :], kn[:, gc * DF_D:(gc + 1) * DF_D])
                                 for gc in range(2 * DF_KV_HEADS)], axis=0)
        s_new = jnp.where(_new_key_mask(rows, n_new, False), s_new, -jnp.inf)

        def pv_new(p):
            return jnp.concatenate([_dot(p[g * 2 * grp:(g + 1) * 2 * grp, :], vn[:, g * DF_V:(g + 1) * DF_V])
                                    for g in range(DF_KV_HEADS)], axis=0)

        _online_update(s_new, m_ref, l_ref, acc_ref, pv_new)
        o = acc_ref[...] / l_ref[...]
        lam = _lam(lq1_ref[...], lk1_ref[...], lq2_ref[...], lk2_ref[...], lam_init)
        for g in range(DF_KV_HEADS):
            d = o[gc_rows(2 * g), :] - lam * o[gc_rows(2 * g + 1), :]
            y = _rms(d, g_ref[...]) * (1.0 - lam_init)
            for h2 in range(2):
                o_ref[2 * g + h2] = y[h2 * DEC_ROWS:(h2 + 1) * DEC_ROWS, :]


def _df_dec(page_table, qdf_s, k_new, v_new, lams, g_df, cache_k_t, cache_v_t, layer, npg, n_new, lam_init):
    db, n_pages = page_table.shape
    n_chunks = n_pages // npg
    rows = 4 * DF_KV_HEADS * DEC_ROWS
    seq = lambda w: pl.BlockSpec((None, DEC_ROWS, w), lambda bi, c, pt: (bi, 0, 0))
    const = lambda shape: pl.BlockSpec(shape, lambda bi, c, pt: (0,) * len(shape))
    in_specs = ([seq(256), seq(LANES), seq(LANES)] + [const((1, DF_D))] * 4 + [const((1, DF_V))]
                + _page_specs((None, None, DF_KV_HEADS, 2, DF_D, LANES), layer, npg, False, n_chunks)
                + _page_specs((None, None, DF_KV_HEADS, DF_V, LANES), layer, npg, False, n_chunks))
    gs = pltpu.PrefetchScalarGridSpec(
        num_scalar_prefetch=1, grid=(db, n_chunks), in_specs=in_specs,
        out_specs=pl.BlockSpec((None, DF_HEADS, DEC_ROWS, DF_V), lambda bi, c, pt: (bi, 0, 0, 0)),
        scratch_shapes=[pltpu.VMEM((2 * DF_KV_HEADS, DF_D, npg * LANES), BF16),
                        pltpu.VMEM((DF_KV_HEADS, DF_V, npg * LANES), BF16),
                        pltpu.VMEM((rows, DF_D), BF16), pltpu.VMEM((rows, 1), F32), pltpu.VMEM((rows, 1), F32),
                        pltpu.VMEM((rows, DF_V), F32)])
    return pl.pallas_call(
        functools.partial(_df_dec_body, npg=npg, n_chunks=n_chunks, n_new=n_new, lam_init=lam_init),
        grid_spec=gs,
        out_shape=jax.ShapeDtypeStruct((db, DF_HEADS, DEC_ROWS, DF_V), F32),
        compiler_params=_cparams(("parallel", "arbitrary")),
        name="df_dec",
    )(page_table, qdf_s, k_new, v_new, *lams, g_df, *([cache_k_t] * npg), *([cache_v_t] * npg))


def _sample_rows(a, bt, db, dt):
    s = a[bt:].reshape(db, dt, a.shape[-1])
    return jnp.pad(s, ((0, 0), (0, DEC_ROWS - dt), (0, 0)))


def _heads_to_rows(o, dt, order=None):
    if order is not None:
        o = o[:, jnp.array(order)]
    db, h, _, w = o.shape
    return jnp.transpose(o[:, :, :dt], (0, 2, 1, 3)).reshape(db * dt, h * w)


_DF_HEAD_ORDER = (0, 2, 1, 3)


def kernel(x_prompt, x_sample, cache_mla_ckv, cache_mla_krope, cache_sb_k, cache_sb_v, cache_df_k, cache_df_v,
           page_table, g_ffn1, w1_gate, w1_up, w1_down, g_mix, w_in, g_q_lat, w_uq, g_kv_lat, w_uk, w_uv,
           g_mla_out, g_sb_out, lam_q1, lam_k1, lam_q2, lam_k2, g_df_sub, w_out, g_ffn2, w2_gate, w2_up, w2_down,
           g_final):
    b, t, d = x_prompt.shape
    db, dt, _ = x_sample.shape
    depth = g_ffn1.shape[0]
    n_pages = page_table.shape[1]
    page = cache_mla_ckv.shape[2]
    assert page == LANES and dt <= DEC_ROWS and d == 1024
    bt, ns = b * t, db * dt
    tq = _pick_tile(t, (256, 128))
    npg = _pick_tile(n_pages, (16, 8, 4, 2, 1))

    pos = jnp.concatenate([jnp.tile(jnp.arange(t, dtype=jnp.int32), b),
                           jnp.tile(n_pages * page + jnp.arange(dt, dtype=jnp.int32), db)])
    inv_freq = ROPE_THETA ** (-jnp.arange(0, DF_D, 2, dtype=F32) / DF_D)
    ang = pos.astype(F32)[:, None] * inv_freq
    cos = jnp.tile(jnp.cos(ang), (1, LANES // 16))
    sin = jnp.tile(jnp.sin(ang), (1, LANES // 16))

    cache_kr_t = jnp.transpose(cache_mla_krope, (0, 1, 3, 2))
    cache_sbk_t = jnp.transpose(cache_sb_k, (0, 1, 3, 4, 2))
    cache_sbv_t = jnp.transpose(cache_sb_v, (0, 1, 3, 4, 2))
    cache_dfk_t = jnp.transpose(cache_df_k, (0, 1, 3, 4, 5, 2))
    cache_dfv_t = jnp.transpose(cache_df_v, (0, 1, 3, 4, 2))

    u_q = jnp.tril(jnp.ones((tq, tq), F32), -1).astype(BF16)
    u_p = jnp.tril(jnp.ones((LANES, LANES), F32), -1).astype(BF16)
    u_8 = jnp.tril(jnp.ones((DEC_ROWS, DEC_ROWS), F32), -1).astype(BF16)

    x = jnp.concatenate([x_prompt.reshape(bt, d), x_sample.reshape(ns, d)], axis=0)
    rows_p = [[] for _ in range(6)]
    rows_s = [[] for _ in range(6)]
    row2 = lambda v: v.reshape(1, -1)
    df_perm = jnp.array([0, 2, 1, 3])

    for l in range(depth):
        lam_init = 0.8 - 0.6 * math.exp(-0.3 * l)
        wi = w_in[l]
        q_df_cols = wi[:, 1696:1952].reshape(d, DF_HEADS, 2 * DF_D)[:, df_perm].reshape(d, 256)
        w_in_r = jnp.concatenate([wi[:, :512], jnp.pad(wi[:, 512:544], ((0, 0), (0, LANES - MLA_ROPE))),
                                  wi[:, 544:1696], q_df_cols, wi[:, 1952:2208]], axis=1).astype(BF16)
        wq = w_uq[l].reshape(256, MLA_HEADS, MLA_NOPE + MLA_ROPE)
        wq_blocks = []
        for hd in range(MLA_HEADS):
            blk = jnp.zeros((256, 256), F32)
            blk = blk.at[:, (hd % 2) * 64:(hd % 2) * 64 + MLA_NOPE].set(wq[:, hd, :MLA_NOPE])
            blk = blk.at[:, LANES:LANES + MLA_ROPE].set(wq[:, hd, MLA_NOPE:])
            wq_blocks.append(blk)
        w_uq_r = jnp.concatenate(wq_blocks, axis=1).astype(BF16)
        w_uk2 = w_uk[l].reshape(256, 384).astype(BF16)
        w_uv2 = w_uv[l].reshape(256, 384).astype(BF16)
        w_ukt = jnp.transpose(w_uk2.reshape(256, 3, LANES), (1, 2, 0))
        wo = w_out[l]
        w_out_r = jnp.concatenate([wo[:768], wo[768:].reshape(DF_HEADS, DF_V, d)[df_perm].reshape(256, d)],
                                  axis=0).astype(BF16)
        lams = (row2(lam_q1[l]), row2(lam_k1[l]), row2(lam_q2[l]), row2(lam_k2[l]))
        g_df = row2(g_df_sub[l])
        g_df2 = jnp.tile(g_df, (1, 2))

        x = _ffn(x, row2(g_ffn1[l]), w1_gate[l].astype(BF16), w1_up[l].astype(BF16), w1_down[l].astype(BF16),
                 row2(g_final), False)
        (ckv, krope, ksb, vsb, kdf, vdf, qm, km, vm, qsb, ksb_b, vsb_b, qdf, kdf_b, vdf_b) = _inproj(
            x, row2(g_mix[l]), w_in_r, row2(g_q_lat[l]), w_uq_r, row2(g_kv_lat[l]), w_uk2, w_uv2, cos, sin)

        o_mla_p = _mla_prompt(qm, km, vm, b, t, tq)
        o_sb_p = _sb_prompt(qsb, ksb_b, vsb_b, u_q, b, t, tq)
        o_df_p = _df_prompt(qdf, kdf_b, vdf_b, lams, g_df2, b, t, tq, lam_init)

        sr = lambda a: _sample_rows(a, bt, db, dt)
        o_mla_s = _mla_dec(page_table, sr(qm), sr(ckv), sr(krope), w_ukt, w_uv2, cache_mla_ckv, cache_kr_t, l, npg, dt)
        o_sb_s = _sb_dec(page_table, sr(qsb), sr(ksb), sr(vsb), u_p, u_8, cache_sbk_t, cache_sbv_t, l, npg, dt)
        o_df_s = _df_dec(page_table, sr(qdf), sr(kdf), sr(vdf), lams, g_df, cache_dfk_t, cache_dfv_t, l, npg, dt,
                         lam_init)

        o_mla = jnp.concatenate([o_mla_p, _heads_to_rows(o_mla_s, dt)], axis=0)
        o_sb = jnp.concatenate([o_sb_p, _heads_to_rows(o_sb_s, dt)], axis=0)
        o_df = jnp.concatenate([o_df_p, _heads_to_rows(o_df_s, dt, _DF_HEAD_ORDER)], axis=0)
        x = _outproj(x, o_mla, o_sb, o_df, row2(g_mla_out[l]), row2(g_sb_out[l]), w_out_r)
        x = _ffn(x, row2(g_ffn2[l]), w2_gate[l].astype(BF16), w2_up[l].astype(BF16), w2_down[l].astype(BF16),
                 row2(g_final), l == depth - 1)

        for i, a in enumerate((ckv, krope, ksb, vsb, kdf, vdf)):
            rows_p[i].append(a[:bt])
            rows_s[i].append(a[bt:])

    def stack(rows, lead, tail):
        return jnp.stack(rows, axis=0).reshape((depth,) + lead + tail)

    tails = ((256,), (MLA_ROPE,), (SB_HEADS, SB_DIM), (SB_HEADS, SB_DIM), (DF_KV_HEADS, 2, DF_D), (DF_KV_HEADS, DF_V))
    outs = [x[:bt].reshape(b, t, d), x[bt:].reshape(db, dt, d)]
    for i in range(6):
        outs.append(stack(rows_p[i], (b, t), tails[i]))
        outs.append(stack(rows_s[i], (db, dt), tails[i]))
    return tuple(outs)
```

```python
import functools
import math

import jax
import jax.numpy as jnp
from jax import lax
from jax.experimental import pallas as pl
from jax.experimental.pallas import tpu as pltpu

F32 = jnp.float32
BF16 = jnp.bfloat16

EPS = 1e-6
ROPE_THETA = 10000.0
MLA_HEADS, MLA_NOPE, MLA_ROPE, MLA_V = 6, 64, 32, 64
SB_HEADS, SB_DIM = 6, 64
DF_HEADS, DF_KV_HEADS, DF_D, DF_V = 4, 2, 32, 64
LOG2E = math.log2(math.e)
MLA_SCALE = (MLA_NOPE + MLA_ROPE) ** -0.5 * LOG2E
DF_SCALE = DF_D ** -0.5 * LOG2E
SB_SCALE = SB_DIM ** -0.5
SB_DEAD = -104.0

LANES = 128
DEC_ROWS = 16
VMEM_LIMIT = 56 * 1024 * 1024
NEG_BIG = -1e30


def _cparams(sem):
    return pltpu.CompilerParams(dimension_semantics=sem, vmem_limit_bytes=VMEM_LIMIT)


def _rms(x, g):
    return x * lax.rsqrt(jnp.mean(x * x, axis=-1, keepdims=True) + EPS) * g


def _dot(a, b):
    return jnp.dot(a, b, preferred_element_type=F32)


def _dot_t(a, b):
    return lax.dot_general(a, b, (((1,), (1,)), ((), ())), preferred_element_type=F32)


def _pick_tile(n, candidates):
    for c in candidates:
        if n % c == 0:
            return c
    raise ValueError(f"no tile for {n}")


def _full(shape):
    nd = len(shape)
    return pl.BlockSpec(shape, lambda *_: (0,) * nd)


def _ffn_body(x_ref, g_ref, wg_ref, wu_ref, wd_ref, gf_ref, o_ref, *, ff_chunk, final_norm):
    x = x_ref[...]
    h = _rms(x, g_ref[...]).astype(BF16)
    acc = jnp.zeros_like(x)
    d_ff = wg_ref.shape[1]
    for c0 in range(0, d_ff, ff_chunk):
        gate = _dot(h, wg_ref[:, c0:c0 + ff_chunk])
        up = _dot(h, wu_ref[:, c0:c0 + ff_chunk])
        act = (gate * jax.nn.sigmoid(gate) * up).astype(BF16)
        acc = acc + _dot(act, wd_ref[c0:c0 + ff_chunk, :])
    y = x + 0.5 * acc
    if final_norm:
        y = _rms(y, gf_ref[...])
    o_ref[...] = y


def _ffn(x, g, wg, wu, wd, g_final, final_norm):
    n, d = x.shape
    d_ff = wg.shape[1]
    tm = _pick_tile(n, (512, 256, 128, 64, 32, 16, 8))
    ff_chunk = d_ff // 2 if (d_ff // 2) % LANES == 0 else d_ff
    row = pl.BlockSpec((tm, d), lambda i: (i, 0))
    return pl.pallas_call(
        functools.partial(_ffn_body, ff_chunk=ff_chunk, final_norm=final_norm),
        grid=(n // tm,),
        in_specs=[row, _full((1, d)), _full((d, d_ff)), _full((d, d_ff)), _full((d_ff, d)), _full((1, d))],
        out_specs=row,
        out_shape=jax.ShapeDtypeStruct((n, d), F32),
        compiler_params=_cparams(("parallel",)),
        name="ffn",
    )(x, g, wg, wu, wd, g_final)


def _rope128(x, cos, sin):
    lane = lax.broadcasted_iota(jnp.int32, x.shape, 1)
    first = (lane % 32) < 16
    rot = jnp.where(first, -pltpu.roll(x, LANES - 16, 1), pltpu.roll(x, 16, 1))
    return x * cos + rot * sin


_CQ, _CKV, _KR, _QSB, _KSB, _VSB, _QDF, _KDF, _VDF, _PROJ_W = 0, 256, 512, 640, 1024, 1408, 1792, 2048, 2176, 2304


def _inproj_body(x_ref, g_ref, win_ref, gq_ref, wuq_ref, gkv_ref, wuk_ref, wuv_ref, cos_ref, sin_ref,
                 ckv_o, krope_o, ksb_o, vsb_o, kdf_o, vdf_o,
                 qm_o, km_o, vm_o, qsb_o, ksbb_o, vsbb_o, qdf_o, kdfb_o, vdfb_o):
    h = _rms(x_ref[...], g_ref[...]).astype(BF16)
    proj = _dot(h, win_ref[...])
    cos = cos_ref[...]
    sin = sin_ref[...]

    cq = _rms(proj[:, _CQ:_CQ + 256], gq_ref[...]).astype(BF16)
    q = _dot(cq, wuq_ref[...])
    for hd in range(MLA_HEADS):
        base = hd * 256
        qm_o[:, base:base + LANES] = (q[:, base:base + LANES] * MLA_SCALE).astype(BF16)
        qr = _rope128(q[:, base + LANES:base + 2 * LANES], cos, sin)
        qm_o[:, base + LANES:base + 2 * LANES] = (qr * MLA_SCALE).astype(BF16)
    ckv = _rms(proj[:, _CKV:_CKV + 256], gkv_ref[...])
    ckv_o[...] = ckv
    ckv_b = ckv.astype(BF16)
    k_nope = _dot(ckv_b, wuk_ref[...])
    vm_o[...] = _dot(ckv_b, wuv_ref[...]).astype(BF16)
    krope = _rope128(proj[:, _KR:_KR + LANES], cos, sin)
    krope_o[...] = krope[:, :MLA_ROPE]
    krope_b = krope.astype(BF16)
    for p in range(MLA_HEADS // 2):
        km_o[:, p * 256:p * 256 + LANES] = k_nope[:, p * LANES:(p + 1) * LANES].astype(BF16)
        km_o[:, p * 256 + LANES:(p + 1) * 256] = krope_b

    qsb_o[...] = (proj[:, _QSB:_QSB + 384] * SB_SCALE).astype(BF16)
    ksb = proj[:, _KSB:_KSB + 384]
    vsb = proj[:, _VSB:_VSB + 384]
    ksb_o[...] = ksb
    vsb_o[...] = vsb
    ksbb_o[...] = ksb.astype(BF16)
    vsbb_o[...] = vsb.astype(BF16)

    for c in range(2):
        qd = _rope128(proj[:, _QDF + c * LANES:_QDF + (c + 1) * LANES], cos, sin)
        qdf_o[:, c * LANES:(c + 1) * LANES] = (qd * DF_SCALE).astype(BF16)
    kdf = _rope128(proj[:, _KDF:_KDF + LANES], cos, sin)
    vdf = proj[:, _VDF:_VDF + LANES]
    kdf_o[...] = kdf
    vdf_o[...] = vdf
    kdfb_o[...] = kdf.astype(BF16)
    vdfb_o[...] = vdf.astype(BF16)


def _inproj(x, g_mix, w_in, g_q, w_uq, g_kv, w_uk, w_uv, cos, sin):
    n, d = x.shape
    tm = _pick_tile(n, (512, 256, 128, 64, 32, 16, 8))
    row = lambda w: pl.BlockSpec((tm, w), lambda i: (i, 0))
    widths_f32 = (256, MLA_ROPE, 384, 384, LANES, LANES)
    widths_bf16 = (MLA_HEADS * 256, 3 * 256, 384, 384, 384, 384, 256, LANES, LANES)
    out_shape = ([jax.ShapeDtypeStruct((n, w), F32) for w in widths_f32]
                 + [jax.ShapeDtypeStruct((n, w), BF16) for w in widths_bf16])
    return pl.pallas_call(
        _inproj_body,
        grid=(n // tm,),
        in_specs=[row(d), _full((1, d)), _full(w_in.shape), _full((1, 256)), _full(w_uq.shape), _full((1, 256)),
                  _full(w_uk.shape), _full(w_uv.shape), row(LANES), row(LANES)],
        out_specs=[row(w) for w in widths_f32 + widths_bf16],
        out_shape=out_shape,
        compiler_params=_cparams(("parallel",)),
        name="inproj",
    )(x, g_mix, w_in, g_q, w_uq, g_kv, w_uk, w_uv, cos, sin)


def _outproj_body(x_ref, om_ref, os_ref, od_ref, gm_ref, gs_ref, w_ref, o_ref):
    mixed_m = _rms(om_ref[...], gm_ref[...]).astype(BF16)
    mixed_s = _rms(os_ref[...], gs_ref[...]).astype(BF16)
    mixed_d = od_ref[...].astype(BF16)
    y = _dot(mixed_m, w_ref[0:384, :]) + _dot(mixed_s, w_ref[384:768, :]) + _dot(mixed_d, w_ref[768:1024, :])
    o_ref[...] = x_ref[...] + y


def _outproj(x, o_mla, o_sb, o_df, g_mla, g_sb, w_out):
    n, d = x.shape
    tm = _pick_tile(n, (512, 256, 128, 64, 32, 16, 8))
    row = lambda w: pl.BlockSpec((tm, w), lambda i: (i, 0))
    return pl.pallas_call(
        _outproj_body,
        grid=(n // tm,),
        in_specs=[row(d), row(384), row(384), row(256), _full((1, 384)), _full((1, 384)), _full(w_out.shape)],
        out_specs=row(d),
        out_shape=jax.ShapeDtypeStruct((n, d), F32),
        compiler_params=_cparams(("parallel",)),
        name="outproj",
    )(x, o_mla, o_sb, o_df, g_mla, g_sb, w_out)


def _softmax_attend(qs, k_ref, v_ref, qi, tq, tk):
    def step(off, carry, masked):
        k = k_ref[pl.ds(off, tk), :]
        v = v_ref[pl.ds(off, tk), :]
        if masked:
            row = lax.broadcasted_iota(jnp.int32, (tq, tk), 0) + qi * tq
            col = lax.broadcasted_iota(jnp.int32, (tq, tk), 1) + off
            valid = col <= row
        out = []
        for q, (m, l, acc) in zip(qs, carry):
            s = _dot_t(q, k)
            if masked:
                s = jnp.where(valid, s, -jnp.inf)
            m_new = jnp.maximum(m, jnp.max(s, axis=-1, keepdims=True))
            p = jnp.exp2(s - m_new)
            alpha = jnp.exp2(m - m_new)
            l = alpha * l + jnp.sum(p, axis=-1, keepdims=True)
            acc = alpha * acc + _dot(p.astype(BF16), v)
            out.append((m_new, l, acc))
        return tuple(out)

    init = tuple((jnp.full((tq, 1), NEG_BIG, F32), jnp.zeros((tq, 1), F32), jnp.zeros((tq, LANES), F32)) for _ in qs)
    n_full = (qi * tq) // tk
    carry = lax.fori_loop(0, n_full, lambda j, c: step(pl.multiple_of(j * tk, tk), c, False), init)
    carry = step(pl.multiple_of(n_full * tk, tk), carry, True)
    return [(acc, l) for _, l, acc in carry]


def _mla_prompt_body(q_ref, k_ref, v_ref, o_ref, *, tq, tk):
    qi = pl.program_id(2)
    lane = lax.broadcasted_iota(jnp.int32, (tq, LANES), 1)
    (acc0, l0), (acc1, l1) = _softmax_attend([q_ref[:, 0:256], q_ref[:, 256:512]], k_ref, v_ref, qi, tq, tk)
    o_ref[...] = jnp.where(lane < 64, acc0 / l0, acc1 / l1)


def _mla_prompt(qm, km, vm, b, t, tq, tk):
    nq = t // tq
    return pl.pallas_call(
        functools.partial(_mla_prompt_body, tq=tq, tk=tk),
        grid=(b, MLA_HEADS // 2, nq),
        in_specs=[pl.BlockSpec((tq, 512), lambda bi, p, qi: (bi * nq + qi, p)),
                  pl.BlockSpec((t, 256), lambda bi, p, qi: (bi, p)),
                  pl.BlockSpec((t, LANES), lambda bi, p, qi: (bi, p))],
        out_specs=pl.BlockSpec((tq, LANES), lambda bi, p, qi: (bi * nq + qi, p)),
        out_shape=jax.ShapeDtypeStruct((b * t, 384), F32),
        compiler_params=_cparams(("parallel", "parallel", "arbitrary")),
        name="mla_prompt",
    )(qm, km, vm)


def _log_keep(z):
    return jnp.minimum(-z, 0.0) - jnp.log1p(jnp.exp(-jnp.abs(z)))


def _suffix_sum(lk, u):
    hi = lk.astype(BF16)
    lo = (lk - hi.astype(F32)).astype(BF16)
    return _dot(hi, u) + _dot(lo, u)


def _sb_weights(z, mask, c, u):
    lk = _log_keep(z)
    if mask is not None:
        lk = jnp.where(mask, lk, 0.0)
    incl = _suffix_sum(lk, u)
    a = jnp.exp(z + incl + c)
    if mask is not None:
        a = jnp.where(mask, a, 0.0)
    return a, c + incl[:, 0:1]


def _any_alive(cs):
    return functools.reduce(jnp.maximum, [jnp.max(c) for c in cs]) > SB_DEAD


def _sb_attend(qs, k_ref, v_ref, u, qi, tq):
    def step(off, carry, mask):
        k = k_ref[pl.ds(off, tq), :]
        v = v_ref[pl.ds(off, tq), :]
        out = []
        for q, (c, acc) in zip(qs, carry):
            a, c = _sb_weights(_dot_t(q, k), mask, c, u)
            out.append((c, acc + _dot(a.astype(BF16), v)))
        return tuple(out)

    row = lax.broadcasted_iota(jnp.int32, (tq, tq), 0)
    col = lax.broadcasted_iota(jnp.int32, (tq, tq), 1)
    init = tuple((jnp.zeros((tq, 1), F32), jnp.zeros((tq, LANES), F32)) for _ in qs)
    carry = step(pl.multiple_of(qi * tq, tq), init, col < row)

    def cond(state):
        i, carry = state
        return jnp.logical_and(i < qi, _any_alive([c for c, _ in carry]))

    def body(state):
        i, carry = state
        return i + 1, step(pl.multiple_of((qi - 1 - i) * tq, tq), carry, None)

    _, carry = lax.while_loop(cond, body, (jnp.int32(0), carry))
    return [acc for _, acc in carry]


def _sb_prompt_body(q_ref, k_ref, v_ref, u_ref, o_ref, *, tq):
    qi = pl.program_id(2)
    lane = lax.broadcasted_iota(jnp.int32, (tq, LANES), 1)
    q = q_ref[...].astype(F32)
    qs = [jnp.where((lane // 64) == hh, q, 0.0).astype(BF16) for hh in range(2)]
    acc0, acc1 = _sb_attend(qs, k_ref, v_ref, u_ref[...], qi, tq)
    o_ref[...] = jnp.where(lane < 64, acc0, acc1)


def _sb_prompt(qsb, ksb, vsb, u, b, t, tq):
    nq = t // tq
    return pl.pallas_call(
        functools.partial(_sb_prompt_body, tq=tq),
        grid=(b, SB_HEADS // 2, nq),
        in_specs=[pl.BlockSpec((tq, LANES), lambda bi, p, qi: (bi * nq + qi, p)),
                  pl.BlockSpec((t, LANES), lambda bi, p, qi: (bi, p)),
                  pl.BlockSpec((t, LANES), lambda bi, p, qi: (bi, p)),
                  _full((tq, tq))],
        out_specs=pl.BlockSpec((tq, LANES), lambda bi, p, qi: (bi * nq + qi, p)),
        out_shape=jax.ShapeDtypeStruct((b * t, 384), F32),
        compiler_params=_cparams(("parallel", "parallel", "arbitrary")),
        name="sb_prompt",
    )(qsb, ksb, vsb, u)


def _lam(lq1, lk1, lq2, lk2, lam_init):
    return (jnp.exp(jnp.sum(lq1 * lk1, axis=-1, keepdims=True))
            - jnp.exp(jnp.sum(lq2 * lk2, axis=-1, keepdims=True)) + lam_init)


def _df_prompt_body(q_ref, k_ref, v_ref, lq1_ref, lk1_ref, lq2_ref, lk2_ref, g_ref, o_ref, *, tq, tk, lam_init):
    qi = pl.program_id(2)
    lane = lax.broadcasted_iota(jnp.int32, (tq, LANES), 1)
    q = q_ref[...].astype(F32)
    lam = _lam(lq1_ref[...], lk1_ref[...], lq2_ref[...], lk2_ref[...], lam_init)
    qs = [jnp.where((lane // DF_D) == gc, q, 0.0).astype(BF16) for gc in range(2 * DF_KV_HEADS)]
    res = _softmax_attend(qs, k_ref, v_ref, qi, tq, tk)
    out = jnp.zeros((tq, LANES), F32)
    for g in range(DF_KV_HEADS):
        (acc1, l1), (acc2, l2) = res[2 * g], res[2 * g + 1]
        own = (lane // 64) == g
        d = jnp.where(own, acc1 / l1 - lam * (acc2 / l2), 0.0)
        ms = jnp.sum(d * d, axis=-1, keepdims=True) * (1.0 / DF_V)
        y = d * lax.rsqrt(ms + EPS) * g_ref[...] * (1.0 - lam_init)
        out = jnp.where(own, y, out)
    o_ref[...] = out


def _df_prompt(qdf, kdf, vdf, lams, g_df2, b, t, tq, tk, lam_init):
    nq = t // tq
    return pl.pallas_call(
        functools.partial(_df_prompt_body, tq=tq, tk=tk, lam_init=lam_init),
        grid=(b, 2, nq),
        in_specs=[pl.BlockSpec((tq, LANES), lambda bi, r, qi: (bi * nq + qi, r)),
                  pl.BlockSpec((t, LANES), lambda bi, r, qi: (bi, 0)),
                  pl.BlockSpec((t, LANES), lambda bi, r, qi: (bi, 0)),
                  _full((1, DF_D)), _full((1, DF_D)), _full((1, DF_D)), _full((1, DF_D)), _full((1, LANES))],
        out_specs=pl.BlockSpec((tq, LANES), lambda bi, r, qi: (bi * nq + qi, r)),
        out_shape=jax.ShapeDtypeStruct((b * t, 256), F32),
        compiler_params=_cparams(("parallel", "parallel", "arbitrary")),
        name="df_prompt",
    )(qdf, kdf, vdf, *lams, g_df2)


def _page_specs(block, layer, first_page, npg, reverse_chunks, n_chunks):
    nz = len(block) - 2

    def imap(bi, c, pt, *, slot):
        chunk = (n_chunks - 1 - c) if reverse_chunks else c
        return (layer, pt[bi, first_page + chunk * npg + slot]) + (0,) * nz

    return [pl.BlockSpec(block, functools.partial(imap, slot=s)) for s in range(npg)]


def _softmax_partial(s, pv):
    m = jnp.max(s, axis=-1, keepdims=True)
    p = jnp.exp2(s - m)
    return m, jnp.sum(p, axis=-1, keepdims=True), pv(p.astype(BF16))


def _merge_partials(parts, m_ref, l_ref, acc_ref):
    m_old = m_ref[...]
    m_new = functools.reduce(jnp.maximum, [m for m, _, _ in parts], m_old)
    w_old = jnp.exp2(m_old - m_new)
    l = w_old * l_ref[...]
    acc = w_old * acc_ref[...]
    for m, lp, ap in parts:
        w = jnp.exp2(m - m_new)
        l = l + w * lp
        acc = acc + w * ap
    m_ref[...] = m_new
    l_ref[...] = l
    acc_ref[...] = acc


def _new_key_mask(rows, n_new, strict):
    t = lax.broadcasted_iota(jnp.int32, (rows, DEC_ROWS), 0) % DEC_ROWS
    kk = lax.broadcasted_iota(jnp.int32, (rows, DEC_ROWS), 1)
    return ((kk < t) if strict else (kk <= t)) & (kk < n_new)


def _head_rows(hd):
    return slice(hd * DEC_ROWS, (hd + 1) * DEC_ROWS)


def _mla_dec_body(pt_ref, q_ref, cnew_ref, rnew_ref, wukt_ref, wuv_ref, *rest, npg, gp, n_chunks, n_new):
    ckv_pages = rest[:npg]
    kr_pages = rest[npg:2 * npg]
    o_ref = rest[2 * npg]
    kvb, krb, qlat, qrope, m_ref, l_ref, acc_ref = rest[2 * npg + 1:]
    c = pl.program_id(1)
    rows = MLA_HEADS * DEC_ROWS

    @pl.when(c == 0)
    def _():
        for hd in range(MLA_HEADS):
            base = hd * 256
            ql = _dot(q_ref[:, base:base + LANES], wukt_ref[hd // 2])
            qlat[_head_rows(hd), :] = ql.astype(BF16)
            qrope[_head_rows(hd), :] = q_ref[:, base + LANES:base + 2 * LANES]
        m_ref[...] = jnp.full(m_ref.shape, NEG_BIG, F32)
        l_ref[...] = jnp.zeros(l_ref.shape, F32)
        acc_ref[...] = jnp.zeros(acc_ref.shape, F32)

    ql = qlat[...]
    qr = qrope[:, :MLA_ROPE]
    parts = []
    for g0 in range(0, npg, gp):
        for i in range(g0, g0 + gp):
            kvb[i * LANES:(i + 1) * LANES, :] = ckv_pages[i][...].astype(BF16)
            krb[:, i * LANES:(i + 1) * LANES] = kr_pages[i][...].astype(BF16)
        span = slice(g0 * LANES, (g0 + gp) * LANES)
        s = _dot_t(ql, kvb[span, :]) + _dot(qr, krb[:, span])
        parts.append(_softmax_partial(s, lambda p, span=span: _dot(p, kvb[span, :])))
    _merge_partials(parts, m_ref, l_ref, acc_ref)

    @pl.when(c == n_chunks - 1)
    def _():
        cn = cnew_ref[...].astype(BF16)
        rn = rnew_ref[...].astype(BF16)
        s_new = _dot_t(ql, cn) + _dot_t(qr, rn)
        s_new = jnp.where(_new_key_mask(rows, n_new, False), s_new, -jnp.inf)
        _merge_partials([_softmax_partial(s_new, lambda p: _dot(p, cn))], m_ref, l_ref, acc_ref)
        o_lat = (acc_ref[...] / l_ref[...]).astype(BF16)
        for hd in range(MLA_HEADS):
            o_ref[hd] = _dot(o_lat[_head_rows(hd), :], wuv_ref[:, hd * MLA_V:(hd + 1) * MLA_V])


def _mla_dec(page_table, qm_s, ckv_new, kr_new, w_ukt, w_uv, cache_ckv, cache_kr_t, layer, npg, gp, n_new):
    db, n_pages = page_table.shape
    n_chunks = n_pages // npg
    rows = MLA_HEADS * DEC_ROWS
    seq = lambda w: pl.BlockSpec((None, DEC_ROWS, w), lambda bi, c, pt: (bi, 0, 0))
    const = lambda shape: pl.BlockSpec(shape, lambda bi, c, pt: (0,) * len(shape))
    in_specs = ([seq(MLA_HEADS * 256), seq(256), seq(MLA_ROPE), const(w_ukt.shape), const(w_uv.shape)]
                + _page_specs((None, None, LANES, 256), layer, 0, npg, False, n_chunks)
                + _page_specs((None, None, MLA_ROPE, LANES), layer, 0, npg, False, n_chunks))
    gs = pltpu.PrefetchScalarGridSpec(
        num_scalar_prefetch=1, grid=(db, n_chunks), in_specs=in_specs,
        out_specs=pl.BlockSpec((None, MLA_HEADS, DEC_ROWS, MLA_V), lambda bi, c, pt: (bi, 0, 0, 0)),
        scratch_shapes=[pltpu.VMEM((npg * LANES, 256), BF16), pltpu.VMEM((MLA_ROPE, npg * LANES), BF16),
                        pltpu.VMEM((rows, 256), BF16), pltpu.VMEM((rows, LANES), BF16),
                        pltpu.VMEM((rows, 1), F32), pltpu.VMEM((rows, 1), F32), pltpu.VMEM((rows, 256), F32)])
    return pl.pallas_call(
        functools.partial(_mla_dec_body, npg=npg, gp=gp, n_chunks=n_chunks, n_new=n_new),
        grid_spec=gs,
        out_shape=jax.ShapeDtypeStruct((db, MLA_HEADS, DEC_ROWS, MLA_V), F32),
        compiler_params=_cparams(("parallel", "arbitrary")),
        name="mla_dec",
    )(page_table, qm_s, ckv_new, kr_new, w_ukt, w_uv, *([cache_ckv] * npg), *([cache_kr_t] * npg))


def _sb_pages(qs, k_pages, v_pages, ktb, vtb, u, c, acc_ref):
    npg = len(k_pages)
    for i in range(npg):
        for hd in range(SB_HEADS):
            ktb[hd, :, i * LANES:(i + 1) * LANES] = k_pages[i][hd].astype(BF16)
            vtb[hd, :, i * LANES:(i + 1) * LANES] = v_pages[i][hd].astype(BF16)
    z = jnp.concatenate([_dot(qs[hd], ktb[hd]) for hd in range(SB_HEADS)], axis=0)
    a = [None] * npg
    for blk in range(npg - 1, -1, -1):
        a[blk], c = _sb_weights(z[:, blk * LANES:(blk + 1) * LANES], None, c, u)
    a = jnp.concatenate(a, axis=1).astype(BF16)
    for hd in range(SB_HEADS):
        acc_ref[_head_rows(hd), :] += _dot_t(a[_head_rows(hd), :], vtb[hd])
    return c


def _sb_dec_new_body(pt_ref, q_ref, knew_ref, vnew_ref, u_ref, un_ref, *rest, npg, n_new):
    k_pages = rest[:npg]
    v_pages = rest[npg:2 * npg]
    acc_o, c_o, ktb, vtb = rest[2 * npg:]
    rows = SB_HEADS * DEC_ROWS
    q = q_ref[...].astype(F32)
    kn = knew_ref[...]
    vn = vnew_ref[...]
    head = lambda x, hd: x[:, hd * SB_DIM:(hd + 1) * SB_DIM].astype(BF16)
    qs = [head(q, hd) for hd in range(SB_HEADS)]
    z = jnp.concatenate([_dot_t(qs[hd], head(kn, hd)) for hd in range(SB_HEADS)], axis=0)
    a, c = _sb_weights(z, _new_key_mask(rows, n_new, True), jnp.zeros((rows, 1), F32), un_ref[...])
    a = a.astype(BF16)
    for hd in range(SB_HEADS):
        acc_o[_head_rows(hd), :] = _dot(a[_head_rows(hd), :], head(vn, hd))
    c_o[...] = _sb_pages(qs, k_pages, v_pages, ktb, vtb, u_ref[...], c, acc_o)


def _sb_dec_old_body(pt_ref, q_ref, accin_ref, cin_ref, u_ref, *rest, npg):
    k_pages = rest[:npg]
    v_pages = rest[npg:2 * npg]
    acc_o, ktb, vtb, c_ref = rest[2 * npg:]
    ci = pl.program_id(1)

    @pl.when(ci == 0)
    def _():
        acc_o[...] = accin_ref[...]
        c_ref[...] = cin_ref[...]

    @pl.when(_any_alive([c_ref[...]]))
    def _():
        q = q_ref[...].astype(F32)
        qs = [q[:, hd * SB_DIM:(hd + 1) * SB_DIM].astype(BF16) for hd in range(SB_HEADS)]
        c_ref[...] = _sb_pages(qs, k_pages, v_pages, ktb, vtb, u_ref[...], c_ref[...], acc_o)


def _sb_dec(page_table, qsb_s, k_new, v_new, u, u_new, cache_k_t, cache_v_t, layer, n_recent, npg_old, n_new):
    db, n_pages = page_table.shape
    rows = SB_HEADS * DEC_ROWS
    page = (None, None, SB_HEADS, SB_DIM, LANES)
    seq = lambda w: pl.BlockSpec((None, DEC_ROWS, w), lambda bi, c, pt: (bi, 0, 0))
    per_seq = lambda w: pl.BlockSpec((None, rows, w), lambda bi, c, pt: (bi, 0, 0))
    const = lambda shape: pl.BlockSpec(shape, lambda bi, c, pt: (0,) * len(shape))
    tb = lambda n: [pltpu.VMEM((SB_HEADS, SB_DIM, n * LANES), BF16)] * 2
    n_old = n_pages - n_recent

    gs = pltpu.PrefetchScalarGridSpec(
        num_scalar_prefetch=1, grid=(db, 1),
        in_specs=([seq(384), seq(384), seq(384), const(u.shape), const(u_new.shape)]
                  + _page_specs(page, layer, n_old, n_recent, False, 1) * 2),
        out_specs=[per_seq(SB_DIM), per_seq(1)],
        scratch_shapes=tb(n_recent))
    acc, c = pl.pallas_call(
        functools.partial(_sb_dec_new_body, npg=n_recent, n_new=n_new),
        grid_spec=gs,
        out_shape=[jax.ShapeDtypeStruct((db, rows, SB_DIM), F32), jax.ShapeDtypeStruct((db, rows, 1), F32)],
        compiler_params=_cparams(("parallel", "arbitrary")),
        name="sb_dec_new",
    )(page_table, qsb_s, k_new, v_new, u, u_new, *([cache_k_t] * n_recent), *([cache_v_t] * n_recent))
    if n_old == 0:
        return acc

    n_chunks = n_old // npg_old
    gs_old = pltpu.PrefetchScalarGridSpec(
        num_scalar_prefetch=1, grid=(db, n_chunks),
        in_specs=([seq(384), per_seq(SB_DIM), per_seq(1), const(u.shape)]
                  + _page_specs(page, layer, 0, npg_old, True, n_chunks) * 2),
        out_specs=per_seq(SB_DIM),
        scratch_shapes=tb(npg_old) + [pltpu.VMEM((rows, 1), F32)])

    def older():
        return pl.pallas_call(
            functools.partial(_sb_dec_old_body, npg=npg_old),
            grid_spec=gs_old,
            out_shape=jax.ShapeDtypeStruct((db, rows, SB_DIM), F32),
            compiler_params=_cparams(("parallel", "arbitrary")),
            name="sb_dec_old",
        )(page_table, qsb_s, acc, c, u, *([cache_k_t] * npg_old), *([cache_v_t] * npg_old))

    return lax.cond(jnp.max(c) > SB_DEAD, older, lambda: acc)


def _df_q_offset(hd, c):
    return (hd % 2) * LANES + (hd // 2) * 64 + c * DF_D


def _df_dec_body(pt_ref, q_ref, knew_ref, vnew_ref, lq1_ref, lk1_ref, lq2_ref, lk2_ref, g_ref, *rest,
                 npg, gp, n_chunks, n_new, lam_init):
    k_pages = rest[:npg]
    v_pages = rest[npg:2 * npg]
    o_ref = rest[2 * npg]
    ktb, vtb, qs, m_ref, l_ref, acc_ref = rest[2 * npg + 1:]
    c = pl.program_id(1)
    grp = 2 * DEC_ROWS
    rows = 2 * DF_KV_HEADS * grp

    @pl.when(c == 0)
    def _():
        q = q_ref[...].astype(F32)
        pieces = []
        for g in range(DF_KV_HEADS):
            for cc in range(2):
                for h2 in range(2):
                    off = _df_q_offset(2 * g + h2, cc)
                    pieces.append(q[:, off:off + DF_D])
        qs[...] = jnp.concatenate(pieces, axis=0).astype(BF16)
        m_ref[...] = jnp.full(m_ref.shape, NEG_BIG, F32)
        l_ref[...] = jnp.zeros(l_ref.shape, F32)
        acc_ref[...] = jnp.zeros(acc_ref.shape, F32)

    def gc_rows(gc):
        return slice(gc * grp, (gc + 1) * grp)

    def kv_rows(g):
        return slice(g * 2 * grp, (g + 1) * 2 * grp)

    parts = []
    for g0 in range(0, npg, gp):
        for i in range(g0, g0 + gp):
            for g in range(DF_KV_HEADS):
                vtb[g, :, i * LANES:(i + 1) * LANES] = v_pages[i][g].astype(BF16)
                for cc in range(2):
                    ktb[2 * g + cc, :, i * LANES:(i + 1) * LANES] = k_pages[i][g, cc].astype(BF16)
        span = slice(g0 * LANES, (g0 + gp) * LANES)
        s = jnp.concatenate([_dot(qs[gc_rows(gc), :], ktb[gc, :, span]) for gc in range(2 * DF_KV_HEADS)], axis=0)

        def pv_pages(p, span=span):
            return jnp.concatenate([_dot_t(p[kv_rows(g), :], vtb[g, :, span]) for g in range(DF_KV_HEADS)], axis=0)

        parts.append(_softmax_partial(s, pv_pages))
    _merge_partials(parts, m_ref, l_ref, acc_ref)

    @pl.when(c == n_chunks - 1)
    def _():
        kn = knew_ref[...].astype(BF16)
        vn = vnew_ref[...].astype(BF16)
        s_new = jnp.concatenate([_dot_t(qs[gc_rows(gc), :], kn[:, gc * DF_D:(gc + 1) * DF_D])
                                 for gc in range(2 * DF_KV_HEADS)], axis=0)
        s_new = jnp.where(_new_key_mask(rows, n_new, False), s_new, -jnp.inf)

        def pv_new(p):
            return jnp.concatenate([_dot(p[kv_rows(g), :], vn[:, g * DF_V:(g + 1) * DF_V])
                                    for g in range(DF_KV_HEADS)], axis=0)

        _merge_partials([_softmax_partial(s_new, pv_new)], m_ref, l_ref, acc_ref)
        o = acc_ref[...] / l_ref[...]
        lam = _lam(lq1_ref[...], lk1_ref[...], lq2_ref[...], lk2_ref[...], lam_init)
        for g in range(DF_KV_HEADS):
            d = o[gc_rows(2 * g), :] - lam * o[gc_rows(2 * g + 1), :]
            y = _rms(d, g_ref[...]) * (1.0 - lam_init)
            for h2 in range(2):
                o_ref[2 * g + h2] = y[_head_rows(h2), :]


def _df_dec(page_table, qdf_s, k_new, v_new, lams, g_df, cache_k_t, cache_v_t, layer, npg, gp, n_new, lam_init):
    db, n_pages = page_table.shape
    n_chunks = n_pages // npg
    rows = 4 * DF_KV_HEADS * DEC_ROWS
    seq = lambda w: pl.BlockSpec((None, DEC_ROWS, w), lambda bi, c, pt: (bi, 0, 0))
    const = lambda shape: pl.BlockSpec(shape, lambda bi, c, pt: (0,) * len(shape))
    in_specs = ([seq(256), seq(LANES), seq(LANES)] + [const((1, DF_D))] * 4 + [const((1, DF_V))]
                + _page_specs((None, None, DF_KV_HEADS, 2, DF_D, LANES), layer, 0, npg, False, n_chunks)
                + _page_specs((None, None, DF_KV_HEADS, DF_V, LANES), layer, 0, npg, False, n_chunks))
    gs = pltpu.PrefetchScalarGridSpec(
        num_scalar_prefetch=1, grid=(db, n_chunks), in_specs=in_specs,
        out_specs=pl.BlockSpec((None, DF_HEADS, DEC_ROWS, DF_V), lambda bi, c, pt: (bi, 0, 0, 0)),
        scratch_shapes=[pltpu.VMEM((2 * DF_KV_HEADS, DF_D, npg * LANES), BF16),
                        pltpu.VMEM((DF_KV_HEADS, DF_V, npg * LANES), BF16),
                        pltpu.VMEM((rows, DF_D), BF16), pltpu.VMEM((rows, 1), F32), pltpu.VMEM((rows, 1), F32),
                        pltpu.VMEM((rows, DF_V), F32)])
    return pl.pallas_call(
        functools.partial(_df_dec_body, npg=npg, gp=gp, n_chunks=n_chunks, n_new=n_new, lam_init=lam_init),
        grid_spec=gs,
        out_shape=jax.ShapeDtypeStruct((db, DF_HEADS, DEC_ROWS, DF_V), F32),
        compiler_params=_cparams(("parallel", "arbitrary")),
        name="df_dec",
    )(page_table, qdf_s, k_new, v_new, *lams, g_df, *([cache_k_t] * npg), *([cache_v_t] * npg))


def _sample_rows(a, bt, db, dt):
    s = a[bt:].reshape(db, dt, a.shape[-1])
    return jnp.pad(s, ((0, 0), (0, DEC_ROWS - dt), (0, 0)))


def _heads_to_rows(o, dt, order=None):
    if order is not None:
        o = o[:, jnp.array(order)]
    db, h, _, w = o.shape
    return jnp.transpose(o[:, :, :dt], (0, 2, 1, 3)).reshape(db * dt, h * w)


_DF_HEAD_ORDER = (0, 2, 1, 3)


def kernel(x_prompt, x_sample, cache_mla_ckv, cache_mla_krope, cache_sb_k, cache_sb_v, cache_df_k, cache_df_v,
           page_table, g_ffn1, w1_gate, w1_up, w1_down, g_mix, w_in, g_q_lat, w_uq, g_kv_lat, w_uk, w_uv,
           g_mla_out, g_sb_out, lam_q1, lam_k1, lam_q2, lam_k2, g_df_sub, w_out, g_ffn2, w2_gate, w2_up, w2_down,
           g_final):
    b, t, d = x_prompt.shape
    db, dt, _ = x_sample.shape
    depth = g_ffn1.shape[0]
    n_pages = page_table.shape[1]
    page = cache_mla_ckv.shape[2]
    assert page == LANES and dt <= DEC_ROWS and d == 1024
    bt, ns = b * t, db * dt
    tq = _pick_tile(t, (256, 128))
    tk = _pick_tile(t, (1024, 512, 256, 128))
    npg = _pick_tile(n_pages, (32, 16, 8, 4, 2, 1))
    gp = npg
    sb_recent = 4 if n_pages >= 8 else max(1, n_pages // 2)
    sb_old = n_pages - sb_recent
    sb_npg_old = max((c for c in range(1, 21) if sb_old % c == 0), default=1)

    pos = jnp.concatenate([jnp.tile(jnp.arange(t, dtype=jnp.int32), b),
                           jnp.tile(n_pages * page + jnp.arange(dt, dtype=jnp.int32), db)])
    inv_freq = ROPE_THETA ** (-jnp.arange(0, DF_D, 2, dtype=F32) / DF_D)
    ang = pos.astype(F32)[:, None] * inv_freq
    cos = jnp.tile(jnp.cos(ang), (1, LANES // 16))
    sin = jnp.tile(jnp.sin(ang), (1, LANES // 16))

    cache_kr_t = jnp.transpose(cache_mla_krope, (0, 1, 3, 2))
    cache_sbk_t = jnp.transpose(cache_sb_k, (0, 1, 3, 4, 2))
    cache_sbv_t = jnp.transpose(cache_sb_v, (0, 1, 3, 4, 2))
    cache_dfk_t = jnp.transpose(cache_df_k, (0, 1, 3, 4, 5, 2))
    cache_dfv_t = jnp.transpose(cache_df_v, (0, 1, 3, 4, 2))

    tri = lambda n: jnp.tril(jnp.ones((n, n), F32)).astype(BF16)
    u_q, u_p, u_new = tri(tq), tri(LANES), tri(DEC_ROWS)

    x = jnp.concatenate([x_prompt.reshape(bt, d), x_sample.reshape(ns, d)], axis=0)
    rows_p = [[] for _ in range(6)]
    rows_s = [[] for _ in range(6)]
    row2 = lambda v: v.reshape(1, -1)
    df_perm = jnp.array([0, 2, 1, 3])

    for l in range(depth):
        lam_init = 0.8 - 0.6 * math.exp(-0.3 * l)
        wi = w_in[l]
        q_df_cols = wi[:, 1696:1952].reshape(d, DF_HEADS, 2 * DF_D)[:, df_perm].reshape(d, 256)
        w_in_r = jnp.concatenate([wi[:, :512], jnp.pad(wi[:, 512:544], ((0, 0), (0, LANES - MLA_ROPE))),
                                  wi[:, 544:1696], q_df_cols, wi[:, 1952:2208]], axis=1).astype(BF16)
        wq = w_uq[l].reshape(256, MLA_HEADS, MLA_NOPE + MLA_ROPE)
        wq_blocks = []
        for hd in range(MLA_HEADS):
            blk = jnp.zeros((256, 256), F32)
            blk = blk.at[:, (hd % 2) * 64:(hd % 2) * 64 + MLA_NOPE].set(wq[:, hd, :MLA_NOPE])
            blk = blk.at[:, LANES:LANES + MLA_ROPE].set(wq[:, hd, MLA_NOPE:])
            wq_blocks.append(blk)
        w_uq_r = jnp.concatenate(wq_blocks, axis=1).astype(BF16)
        w_uk2 = w_uk[l].reshape(256, 384).astype(BF16)
        w_uv2 = w_uv[l].reshape(256, 384).astype(BF16)
        w_ukt = jnp.transpose(w_uk2.reshape(256, 3, LANES), (1, 2, 0))
        wo = w_out[l]
        w_out_r = jnp.concatenate([wo[:768], wo[768:].reshape(DF_HEADS, DF_V, d)[df_perm].reshape(256, d)],
                                  axis=0).astype(BF16)
        lams = (row2(lam_q1[l]), row2(lam_k1[l]), row2(lam_q2[l]), row2(lam_k2[l]))
        g_df = row2(g_df_sub[l])
        g_df2 = jnp.tile(g_df, (1, 2))

        x = _ffn(x, row2(g_ffn1[l]), w1_gate[l].astype(BF16), w1_up[l].astype(BF16), w1_down[l].astype(BF16),
                 row2(g_final), False)
        (ckv, krope, ksb, vsb, kdf, vdf, qm, km, vm, qsb, ksb_b, vsb_b, qdf, kdf_b, vdf_b) = _inproj(
            x, row2(g_mix[l]), w_in_r, row2(g_q_lat[l]), w_uq_r, row2(g_kv_lat[l]), w_uk2, w_uv2, cos, sin)

        o_mla_p = _mla_prompt(qm, km, vm, b, t, tq, tk)
        o_sb_p = _sb_prompt(qsb, ksb_b, vsb_b, u_q, b, t, tq)
        o_df_p = _df_prompt(qdf, kdf_b, vdf_b, lams, g_df2, b, t, tq, tk, lam_init)

        sr = lambda a: _sample_rows(a, bt, db, dt)
        o_mla_s = _mla_dec(page_table, sr(qm), sr(ckv), sr(krope), w_ukt, w_uv2, cache_mla_ckv, cache_kr_t, l, npg, gp,
                           dt)
        o_sb_s = _sb_dec(page_table, sr(qsb), sr(ksb), sr(vsb), u_p, u_new, cache_sbk_t, cache_sbv_t, l, sb_recent,
                         sb_npg_old, dt).reshape(db, SB_HEADS, DEC_ROWS, SB_DIM)
        o_df_s = _df_dec(page_table, sr(qdf), sr(kdf), sr(vdf), lams, g_df, cache_dfk_t, cache_dfv_t, l, npg, gp, dt,
                         lam_init)

        o_mla = jnp.concatenate([o_mla_p, _heads_to_rows(o_mla_s, dt)], axis=0)
        o_sb = jnp.concatenate([o_sb_p, _heads_to_rows(o_sb_s, dt)], axis=0)
        o_df = jnp.concatenate([o_df_p, _heads_to_rows(o_df_s, dt, _DF_HEAD_ORDER)], axis=0)
        x = _outproj(x, o_mla, o_sb, o_df, row2(g_mla_out[l]), row2(g_sb_out[l]), w_out_r)
        x = _ffn(x, row2(g_ffn2[l]), w2_gate[l].astype(BF16), w2_up[l].astype(BF16), w2_down[l].astype(BF16),
                 row2(g_final), l == depth - 1)

        for i, a in enumerate((ckv, krope, ksb, vsb, kdf, vdf)):
            rows_p[i].append(a[:bt])
            rows_s[i].append(a[bt:])

    def stack(rows, lead, tail):
        return jnp.stack(rows, axis=0).reshape((depth,) + lead + tail)

    tails = ((256,), (MLA_ROPE,), (SB_HEADS, SB_DIM), (SB_HEADS, SB_DIM), (DF_KV_HEADS, 2, DF_D), (DF_KV_HEADS, DF_V))
    outs = [x[:bt].reshape(b, t, d), x[bt:].reshape(db, dt, d)]
    for i in range(6):
        outs.append(stack(rows_p[i], (b, t), tails[i]))
        outs.append(stack(rows_s[i], (db, dt), tails[i]))
    return tuple(outs)
```

```python
import functools
import math

import jax
import jax.numpy as jnp
from jax import lax
from jax.experimental import pallas as pl
from jax.experimental.pallas import tpu as pltpu

F32 = jnp.float32
BF16 = jnp.bfloat16

EPS = 1e-6
ROPE_THETA = 10000.0
MLA_HEADS, MLA_NOPE, MLA_ROPE, MLA_V = 6, 64, 32, 64
SB_HEADS, SB_DIM = 6, 64
DF_HEADS, DF_KV_HEADS, DF_D, DF_V = 4, 2, 32, 64
LOG2E = math.log2(math.e)
MLA_SCALE = (MLA_NOPE + MLA_ROPE) ** -0.5 * LOG2E
DF_SCALE = DF_D ** -0.5 * LOG2E
SB_SCALE = SB_DIM ** -0.5
SB_DEAD = -104.0

LANES = 128
DEC_ROWS = 16
VMEM_LIMIT = 56 * 1024 * 1024
NEG_BIG = -1e30


def _cparams(sem):
    return pltpu.CompilerParams(dimension_semantics=sem, vmem_limit_bytes=VMEM_LIMIT)


def _rms(x, g):
    return x * lax.rsqrt(jnp.mean(x * x, axis=-1, keepdims=True) + EPS) * g


def _dot(a, b):
    return jnp.dot(a, b, preferred_element_type=F32)


def _dot_t(a, b):
    return lax.dot_general(a, b, (((1,), (1,)), ((), ())), preferred_element_type=F32)


def _pick_tile(n, candidates):
    for c in candidates:
        if n % c == 0:
            return c
    raise ValueError(f"no tile for {n}")


def _full(shape):
    nd = len(shape)
    return pl.BlockSpec(shape, lambda *_: (0,) * nd)


def _ffn_body(x_ref, g_ref, wg_ref, wu_ref, wd_ref, gf_ref, o_ref, *, ff_chunk, final_norm):
    x = x_ref[...]
    h = _rms(x, g_ref[...]).astype(BF16)
    acc = jnp.zeros_like(x)
    d_ff = wg_ref.shape[1]
    for c0 in range(0, d_ff, ff_chunk):
        gate = _dot(h, wg_ref[:, c0:c0 + ff_chunk])
        up = _dot(h, wu_ref[:, c0:c0 + ff_chunk])
        act = (gate * jax.nn.sigmoid(gate) * up).astype(BF16)
        acc = acc + _dot(act, wd_ref[c0:c0 + ff_chunk, :])
    y = x + 0.5 * acc
    if final_norm:
        y = _rms(y, gf_ref[...])
    o_ref[...] = y


def _ffn(x, g, wg, wu, wd, g_final, final_norm):
    n, d = x.shape
    d_ff = wg.shape[1]
    tm = _pick_tile(n, (512, 256, 128, 64, 32, 16, 8))
    ff_chunk = d_ff // 2 if (d_ff // 2) % LANES == 0 else d_ff
    row = pl.BlockSpec((tm, d), lambda i: (i, 0))
    return pl.pallas_call(
        functools.partial(_ffn_body, ff_chunk=ff_chunk, final_norm=final_norm),
        grid=(n // tm,),
        in_specs=[row, _full((1, d)), _full((d, d_ff)), _full((d, d_ff)), _full((d_ff, d)), _full((1, d))],
        out_specs=row,
        out_shape=jax.ShapeDtypeStruct((n, d), F32),
        compiler_params=_cparams(("parallel",)),
        name="ffn",
    )(x, g, wg, wu, wd, g_final)


def _rope128(x, cos, sin):
    lane = lax.broadcasted_iota(jnp.int32, x.shape, 1)
    first = (lane % 32) < 16
    rot = jnp.where(first, -pltpu.roll(x, LANES - 16, 1), pltpu.roll(x, 16, 1))
    return x * cos + rot * sin


_CQ, _CKV, _KR, _QSB, _KSB, _VSB, _QDF, _KDF, _VDF, _PROJ_W = 0, 256, 512, 640, 1024, 1408, 1792, 2048, 2176, 2304


def _inproj_body(x_ref, g_ref, win_ref, gq_ref, wuq_ref, gkv_ref, wuk_ref, wuv_ref, cos_ref, sin_ref,
                 ckv_o, krope_o, ksb_o, vsb_o, kdf_o, vdf_o,
                 qm_o, km_o, vm_o, qsb_o, ksbb_o, vsbb_o, qdf_o, kdfb_o, vdfb_o):
    h = _rms(x_ref[...], g_ref[...]).astype(BF16)
    proj = _dot(h, win_ref[...])
    cos = cos_ref[...]
    sin = sin_ref[...]

    cq = _rms(proj[:, _CQ:_CQ + 256], gq_ref[...]).astype(BF16)
    q = _dot(cq, wuq_ref[...])
    for hd in range(MLA_HEADS):
        base = hd * 256
        qm_o[:, base:base + LANES] = (q[:, base:base + LANES] * MLA_SCALE).astype(BF16)
        qr = _rope128(q[:, base + LANES:base + 2 * LANES], cos, sin)
        qm_o[:, base + LANES:base + 2 * LANES] = (qr * MLA_SCALE).astype(BF16)
    ckv = _rms(proj[:, _CKV:_CKV + 256], gkv_ref[...])
    ckv_o[...] = ckv
    ckv_b = ckv.astype(BF16)
    k_nope = _dot(ckv_b, wuk_ref[...])
    vm_o[...] = _dot(ckv_b, wuv_ref[...]).astype(BF16)
    krope = _rope128(proj[:, _KR:_KR + LANES], cos, sin)
    krope_o[...] = krope[:, :MLA_ROPE]
    krope_b = krope.astype(BF16)
    for p in range(MLA_HEADS // 2):
        km_o[:, p * 256:p * 256 + LANES] = k_nope[:, p * LANES:(p + 1) * LANES].astype(BF16)
        km_o[:, p * 256 + LANES:(p + 1) * 256] = krope_b

    qsb_o[...] = (proj[:, _QSB:_QSB + 384] * SB_SCALE).astype(BF16)
    ksb = proj[:, _KSB:_KSB + 384]
    vsb = proj[:, _VSB:_VSB + 384]
    ksb_o[...] = ksb
    vsb_o[...] = vsb
    ksbb_o[...] = ksb.astype(BF16)
    vsbb_o[...] = vsb.astype(BF16)

    for c in range(2):
        qd = _rope128(proj[:, _QDF + c * LANES:_QDF + (c + 1) * LANES], cos, sin)
        qdf_o[:, c * LANES:(c + 1) * LANES] = (qd * DF_SCALE).astype(BF16)
    kdf = _rope128(proj[:, _KDF:_KDF + LANES], cos, sin)
    vdf = proj[:, _VDF:_VDF + LANES]
    kdf_o[...] = kdf
    vdf_o[...] = vdf
    kdfb_o[...] = kdf.astype(BF16)
    vdfb_o[...] = vdf.astype(BF16)


def _inproj(x, g_mix, w_in, g_q, w_uq, g_kv, w_uk, w_uv, cos, sin):
    n, d = x.shape
    tm = _pick_tile(n, (512, 256, 128, 64, 32, 16, 8))
    row = lambda w: pl.BlockSpec((tm, w), lambda i: (i, 0))
    widths_f32 = (256, MLA_ROPE, 384, 384, LANES, LANES)
    widths_bf16 = (MLA_HEADS * 256, 3 * 256, 384, 384, 384, 384, 256, LANES, LANES)
    out_shape = ([jax.ShapeDtypeStruct((n, w), F32) for w in widths_f32]
                 + [jax.ShapeDtypeStruct((n, w), BF16) for w in widths_bf16])
    return pl.pallas_call(
        _inproj_body,
        grid=(n // tm,),
        in_specs=[row(d), _full((1, d)), _full(w_in.shape), _full((1, 256)), _full(w_uq.shape), _full((1, 256)),
                  _full(w_uk.shape), _full(w_uv.shape), row(LANES), row(LANES)],
        out_specs=[row(w) for w in widths_f32 + widths_bf16],
        out_shape=out_shape,
        compiler_params=_cparams(("parallel",)),
        name="inproj",
    )(x, g_mix, w_in, g_q, w_uq, g_kv, w_uk, w_uv, cos, sin)


def _outproj_body(x_ref, om_ref, os_ref, od_ref, gm_ref, gs_ref, w_ref, o_ref):
    mixed_m = _rms(om_ref[...], gm_ref[...]).astype(BF16)
    mixed_s = _rms(os_ref[...], gs_ref[...]).astype(BF16)
    mixed_d = od_ref[...].astype(BF16)
    y = _dot(mixed_m, w_ref[0:384, :]) + _dot(mixed_s, w_ref[384:768, :]) + _dot(mixed_d, w_ref[768:1024, :])
    o_ref[...] = x_ref[...] + y


def _outproj(x, o_mla, o_sb, o_df, g_mla, g_sb, w_out):
    n, d = x.shape
    tm = _pick_tile(n, (512, 256, 128, 64, 32, 16, 8))
    row = lambda w: pl.BlockSpec((tm, w), lambda i: (i, 0))
    return pl.pallas_call(
        _outproj_body,
        grid=(n // tm,),
        in_specs=[row(d), row(384), row(384), row(256), _full((1, 384)), _full((1, 384)), _full(w_out.shape)],
        out_specs=row(d),
        out_shape=jax.ShapeDtypeStruct((n, d), F32),
        compiler_params=_cparams(("parallel",)),
        name="outproj",
    )(x, o_mla, o_sb, o_df, g_mla, g_sb, w_out)


def _softmax_attend(qs, k_ref, v_ref, qi, tq, tk):
    def step(off, carry, masked):
        k = k_ref[pl.ds(off, tk), :]
        v = v_ref[pl.ds(off, tk), :]
        if masked:
            row = lax.broadcasted_iota(jnp.int32, (tq, tk), 0) + qi * tq
            col = lax.broadcasted_iota(jnp.int32, (tq, tk), 1) + off
            valid = col <= row
        out = []
        for q, (m, l, acc) in zip(qs, carry):
            s = _dot_t(q, k)
            if masked:
                s = jnp.where(valid, s, -jnp.inf)
            m_new = jnp.maximum(m, jnp.max(s, axis=-1, keepdims=True))
            p = jnp.exp2(s - m_new)
            alpha = jnp.exp2(m - m_new)
            l = alpha * l + jnp.sum(p, axis=-1, keepdims=True)
            acc = alpha * acc + _dot(p.astype(BF16), v)
            out.append((m_new, l, acc))
        return tuple(out)

    init = tuple((jnp.full((tq, 1), NEG_BIG, F32), jnp.zeros((tq, 1), F32), jnp.zeros((tq, LANES), F32)) for _ in qs)
    n_full = (qi * tq) // tk
    carry = lax.fori_loop(0, n_full, lambda j, c: step(pl.multiple_of(j * tk, tk), c, False), init)
    carry = step(pl.multiple_of(n_full * tk, tk), carry, True)
    return [(acc, l) for _, l, acc in carry]


def _mla_prompt_body(q_ref, k_ref, v_ref, o_ref, *, tq, tk):
    qi = pl.program_id(2)
    lane = lax.broadcasted_iota(jnp.int32, (tq, LANES), 1)
    (acc0, l0), (acc1, l1) = _softmax_attend([q_ref[:, 0:256], q_ref[:, 256:512]], k_ref, v_ref, qi, tq, tk)
    o_ref[...] = jnp.where(lane < 64, acc0 / l0, acc1 / l1)


def _mla_prompt(qm, km, vm, b, t, tq, tk):
    nq = t // tq
    return pl.pallas_call(
        functools.partial(_mla_prompt_body, tq=tq, tk=tk),
        grid=(b, MLA_HEADS // 2, nq),
        in_specs=[pl.BlockSpec((tq, 512), lambda bi, p, qi: (bi * nq + qi, p)),
                  pl.BlockSpec((t, 256), lambda bi, p, qi: (bi, p)),
                  pl.BlockSpec((t, LANES), lambda bi, p, qi: (bi, p))],
        out_specs=pl.BlockSpec((tq, LANES), lambda bi, p, qi: (bi * nq + qi, p)),
        out_shape=jax.ShapeDtypeStruct((b * t, 384), F32),
        compiler_params=_cparams(("parallel", "parallel", "arbitrary")),
        name="mla_prompt",
    )(qm, km, vm)


def _log_keep(z):
    return jnp.minimum(-z, 0.0) - jnp.log1p(jnp.exp(-jnp.abs(z)))


def _suffix_sum(lk, u):
    hi = lk.astype(BF16)
    lo = (lk - hi.astype(F32)).astype(BF16)
    return _dot(hi, u) + _dot(lo, u)


def _sb_weights(z, mask, c, u):
    lk = _log_keep(z)
    if mask is not None:
        lk = jnp.where(mask, lk, 0.0)
    incl = _suffix_sum(lk, u)
    a = jnp.exp(z + incl + c)
    if mask is not None:
        a = jnp.where(mask, a, 0.0)
    return a, c + incl[:, 0:1]


def _any_alive(cs):
    return functools.reduce(jnp.maximum, [jnp.max(c) for c in cs]) > SB_DEAD


def _sb_attend(qs, k_ref, v_ref, u, qi, tq):
    def step(off, carry, mask):
        k = k_ref[pl.ds(off, tq), :]
        v = v_ref[pl.ds(off, tq), :]
        out = []
        for q, (c, acc) in zip(qs, carry):
            a, c = _sb_weights(_dot_t(q, k), mask, c, u)
            out.append((c, acc + _dot(a.astype(BF16), v)))
        return tuple(out)

    row = lax.broadcasted_iota(jnp.int32, (tq, tq), 0)
    col = lax.broadcasted_iota(jnp.int32, (tq, tq), 1)
    init = tuple((jnp.zeros((tq, 1), F32), jnp.zeros((tq, LANES), F32)) for _ in qs)
    carry = step(pl.multiple_of(qi * tq, tq), init, col < row)

    def cond(state):
        i, carry = state
        return jnp.logical_and(i < qi, _any_alive([c for c, _ in carry]))

    def body(state):
        i, carry = state
        return i + 1, step(pl.multiple_of((qi - 1 - i) * tq, tq), carry, None)

    _, carry = lax.while_loop(cond, body, (jnp.int32(0), carry))
    return [acc for _, acc in carry]


def _sb_prompt_body(q_ref, k_ref, v_ref, u_ref, o_ref, *, tq):
    qi = pl.program_id(2)
    lane = lax.broadcasted_iota(jnp.int32, (tq, LANES), 1)
    q = q_ref[...].astype(F32)
    qs = [jnp.where((lane // 64) == hh, q, 0.0).astype(BF16) for hh in range(2)]
    acc0, acc1 = _sb_attend(qs, k_ref, v_ref, u_ref[...], qi, tq)
    o_ref[...] = jnp.where(lane < 64, acc0, acc1)


def _sb_prompt(qsb, ksb, vsb, u, b, t, tq):
    nq = t // tq
    return pl.pallas_call(
        functools.partial(_sb_prompt_body, tq=tq),
        grid=(b, SB_HEADS // 2, nq),
        in_specs=[pl.BlockSpec((tq, LANES), lambda bi, p, qi: (bi * nq + qi, p)),
                  pl.BlockSpec((t, LANES), lambda bi, p, qi: (bi, p)),
                  pl.BlockSpec((t, LANES), lambda bi, p, qi: (bi, p)),
                  _full((tq, tq))],
        out_specs=pl.BlockSpec((tq, LANES), lambda bi, p, qi: (bi * nq + qi, p)),
        out_shape=jax.ShapeDtypeStruct((b * t, 384), F32),
        compiler_params=_cparams(("parallel", "parallel", "arbitrary")),
        name="sb_prompt",
    )(qsb, ksb, vsb, u)


def _lam(lq1, lk1, lq2, lk2, lam_init):
    return (jnp.exp(jnp.sum(lq1 * lk1, axis=-1, keepdims=True))
            - jnp.exp(jnp.sum(lq2 * lk2, axis=-1, keepdims=True)) + lam_init)


def _df_prompt_body(q_ref, k_ref, v_ref, lq1_ref, lk1_ref, lq2_ref, lk2_ref, g_ref, o_ref, *, tq, tk, lam_init):
    qi = pl.program_id(2)
    lane = lax.broadcasted_iota(jnp.int32, (tq, LANES), 1)
    q = q_ref[...].astype(F32)
    lam = _lam(lq1_ref[...], lk1_ref[...], lq2_ref[...], lk2_ref[...], lam_init)
    qs = [jnp.where((lane // DF_D) == gc, q, 0.0).astype(BF16) for gc in range(2 * DF_KV_HEADS)]
    res = _softmax_attend(qs, k_ref, v_ref, qi, tq, tk)
    out = jnp.zeros((tq, LANES), F32)
    for g in range(DF_KV_HEADS):
        (acc1, l1), (acc2, l2) = res[2 * g], res[2 * g + 1]
        own = (lane // 64) == g
        d = jnp.where(own, acc1 / l1 - lam * (acc2 / l2), 0.0)
        ms = jnp.sum(d * d, axis=-1, keepdims=True) * (1.0 / DF_V)
        y = d * lax.rsqrt(ms + EPS) * g_ref[...] * (1.0 - lam_init)
        out = jnp.where(own, y, out)
    o_ref[...] = out


def _df_prompt(qdf, kdf, vdf, lams, g_df2, b, t, tq, tk, lam_init):
    nq = t // tq
    return pl.pallas_call(
        functools.partial(_df_prompt_body, tq=tq, tk=tk, lam_init=lam_init),
        grid=(b, 2, nq),
        in_specs=[pl.BlockSpec((tq, LANES), lambda bi, r, qi: (bi * nq + qi, r)),
                  pl.BlockSpec((t, LANES), lambda bi, r, qi: (bi, 0)),
                  pl.BlockSpec((t, LANES), lambda bi, r, qi: (bi, 0)),
                  _full((1, DF_D)), _full((1, DF_D)), _full((1, DF_D)), _full((1, DF_D)), _full((1, LANES))],
        out_specs=pl.BlockSpec((tq, LANES), lambda bi, r, qi: (bi * nq + qi, r)),
        out_shape=jax.ShapeDtypeStruct((b * t, 256), F32),
        compiler_params=_cparams(("parallel", "parallel", "arbitrary")),
        name="df_prompt",
    )(qdf, kdf, vdf, *lams, g_df2)


def _page_specs(block, layer, first_page, npg, reverse_chunks, n_chunks):
    nz = len(block) - 2

    def imap(bi, c, pt, *, slot):
        chunk = (n_chunks - 1 - c) if reverse_chunks else c
        return (layer, pt[bi, first_page + chunk * npg + slot]) + (0,) * nz

    return [pl.BlockSpec(block, functools.partial(imap, slot=s)) for s in range(npg)]


def _softmax_partial(s, pv):
    m = jnp.max(s, axis=-1, keepdims=True)
    p = jnp.exp2(s - m)
    return m, jnp.sum(p, axis=-1, keepdims=True), pv(p.astype(BF16))


def _merge_partials(parts, m_ref, l_ref, acc_ref):
    m_old = m_ref[...]
    m_new = functools.reduce(jnp.maximum, [m for m, _, _ in parts], m_old)
    w_old = jnp.exp2(m_old - m_new)
    l = w_old * l_ref[...]
    acc = w_old * acc_ref[...]
    for m, lp, ap in parts:
        w = jnp.exp2(m - m_new)
        l = l + w * lp
        acc = acc + w * ap
    m_ref[...] = m_new
    l_ref[...] = l
    acc_ref[...] = acc


def _new_key_mask(rows, n_new, strict):
    t = lax.broadcasted_iota(jnp.int32, (rows, DEC_ROWS), 0) % DEC_ROWS
    kk = lax.broadcasted_iota(jnp.int32, (rows, DEC_ROWS), 1)
    return ((kk < t) if strict else (kk <= t)) & (kk < n_new)


def _head_rows(hd):
    return slice(hd * DEC_ROWS, (hd + 1) * DEC_ROWS)


def _mla_dec_body(pt_ref, q_ref, cnew_ref, rnew_ref, wukt_ref, wuv_ref, *rest, npg, gp, n_chunks, n_new):
    ckv_pages = rest[:npg]
    kr_pages = rest[npg:2 * npg]
    o_ref = rest[2 * npg]
    kvb, krb, qlat, qrope, m_ref, l_ref, acc_ref = rest[2 * npg + 1:]
    c = pl.program_id(1)
    rows = MLA_HEADS * DEC_ROWS

    @pl.when(c == 0)
    def _():
        for hd in range(MLA_HEADS):
            base = hd * 256
            ql = _dot(q_ref[:, base:base + LANES], wukt_ref[hd // 2])
            qlat[_head_rows(hd), :] = ql.astype(BF16)
            qrope[_head_rows(hd), :] = q_ref[:, base + LANES:base + 2 * LANES]
        m_ref[...] = jnp.full(m_ref.shape, NEG_BIG, F32)
        l_ref[...] = jnp.zeros(l_ref.shape, F32)
        acc_ref[...] = jnp.zeros(acc_ref.shape, F32)

    ql = qlat[...]
    qr = qrope[:, :MLA_ROPE]
    parts = []
    for g0 in range(0, npg, gp):
        for i in range(g0, g0 + gp):
            kvb[i * LANES:(i + 1) * LANES, :] = ckv_pages[i][...].astype(BF16)
            krb[:, i * LANES:(i + 1) * LANES] = kr_pages[i][...].astype(BF16)
        span = slice(g0 * LANES, (g0 + gp) * LANES)
        s = _dot_t(ql, kvb[span, :]) + _dot(qr, krb[:, span])
        parts.append(_softmax_partial(s, lambda p, span=span: _dot(p, kvb[span, :])))
    _merge_partials(parts, m_ref, l_ref, acc_ref)

    @pl.when(c == n_chunks - 1)
    def _():
        cn = cnew_ref[...].astype(BF16)
        rn = rnew_ref[...].astype(BF16)
        s_new = _dot_t(ql, cn) + _dot_t(qr, rn)
        s_new = jnp.where(_new_key_mask(rows, n_new, False), s_new, -jnp.inf)
        _merge_partials([_softmax_partial(s_new, lambda p: _dot(p, cn))], m_ref, l_ref, acc_ref)
        o_lat = (acc_ref[...] / l_ref[...]).astype(BF16)
        for hd in range(MLA_HEADS):
            o_ref[hd] = _dot(o_lat[_head_rows(hd), :], wuv_ref[:, hd * MLA_V:(hd + 1) * MLA_V])


def _mla_dec(page_table, qm_s, ckv_new, kr_new, w_ukt, w_uv, cache_ckv, cache_kr_t, layer, npg, gp, n_new):
    db, n_pages = page_table.shape
    n_chunks = n_pages // npg
    rows = MLA_HEADS * DEC_ROWS
    seq = lambda w: pl.BlockSpec((None, DEC_ROWS, w), lambda bi, c, pt: (bi, 0, 0))
    const = lambda shape: pl.BlockSpec(shape, lambda bi, c, pt: (0,) * len(shape))
    in_specs = ([seq(MLA_HEADS * 256), seq(256), seq(MLA_ROPE), const(w_ukt.shape), const(w_uv.shape)]
                + _page_specs((None, None, LANES, 256), layer, 0, npg, False, n_chunks)
                + _page_specs((None, None, MLA_ROPE, LANES), layer, 0, npg, False, n_chunks))
    gs = pltpu.PrefetchScalarGridSpec(
        num_scalar_prefetch=1, grid=(db, n_chunks), in_specs=in_specs,
        out_specs=pl.BlockSpec((None, MLA_HEADS, DEC_ROWS, MLA_V), lambda bi, c, pt: (bi, 0, 0, 0)),
        scratch_shapes=[pltpu.VMEM((npg * LANES, 256), BF16), pltpu.VMEM((MLA_ROPE, npg * LANES), BF16),
                        pltpu.VMEM((rows, 256), BF16), pltpu.VMEM((rows, LANES), BF16),
                        pltpu.VMEM((rows, 1), F32), pltpu.VMEM((rows, 1), F32), pltpu.VMEM((rows, 256), F32)])
    return pl.pallas_call(
        functools.partial(_mla_dec_body, npg=npg, gp=gp, n_chunks=n_chunks, n_new=n_new),
        grid_spec=gs,
        out_shape=jax.ShapeDtypeStruct((db, MLA_HEADS, DEC_ROWS, MLA_V), F32),
        compiler_params=_cparams(("parallel", "arbitrary")),
        name="mla_dec",
    )(page_table, qm_s, ckv_new, kr_new, w_ukt, w_uv, *([cache_ckv] * npg), *([cache_kr_t] * npg))


def _sb_pages(qs, k_pages, v_pages, ktb, vtb, u, c, acc_ref):
    npg = len(k_pages)
    for i in range(npg):
        for hd in range(SB_HEADS):
            ktb[hd, :, i * LANES:(i + 1) * LANES] = k_pages[i][hd].astype(BF16)
            vtb[hd, :, i * LANES:(i + 1) * LANES] = v_pages[i][hd].astype(BF16)
    z = jnp.concatenate([_dot(qs[hd], ktb[hd]) for hd in range(SB_HEADS)], axis=0)
    a = [None] * npg
    for blk in range(npg - 1, -1, -1):
        a[blk], c = _sb_weights(z[:, blk * LANES:(blk + 1) * LANES], None, c, u)
    a = jnp.concatenate(a, axis=1).astype(BF16)
    for hd in range(SB_HEADS):
        acc_ref[_head_rows(hd), :] += _dot_t(a[_head_rows(hd), :], vtb[hd])
    return c


def _sb_dec_new_body(pt_ref, q_ref, knew_ref, vnew_ref, u_ref, un_ref, *rest, npg, n_new):
    k_pages = rest[:npg]
    v_pages = rest[npg:2 * npg]
    acc_o, c_o, ktb, vtb = rest[2 * npg:]
    rows = SB_HEADS * DEC_ROWS
    q = q_ref[...].astype(F32)
    kn = knew_ref[...]
    vn = vnew_ref[...]
    head = lambda x, hd: x[:, hd * SB_DIM:(hd + 1) * SB_DIM].astype(BF16)
    qs = [head(q, hd) for hd in range(SB_HEADS)]
    z = jnp.concatenate([_dot_t(qs[hd], head(kn, hd)) for hd in range(SB_HEADS)], axis=0)
    a, c = _sb_weights(z, _new_key_mask(rows, n_new, True), jnp.zeros((rows, 1), F32), un_ref[...])
    a = a.astype(BF16)
    for hd in range(SB_HEADS):
        acc_o[_head_rows(hd), :] = _dot(a[_head_rows(hd), :], head(vn, hd))
    c_o[...] = _sb_pages(qs, k_pages, v_pages, ktb, vtb, u_ref[...], c, acc_o)


def _sb_dec_old_body(pt_ref, q_ref, accin_ref, cin_ref, u_ref, *rest, npg):
    k_pages = rest[:npg]
    v_pages = rest[npg:2 * npg]
    acc_o, ktb, vtb, c_ref = rest[2 * npg:]
    ci = pl.program_id(1)

    @pl.when(ci == 0)
    def _():
        acc_o[...] = accin_ref[...]
        c_ref[...] = cin_ref[...]

    @pl.when(_any_alive([c_ref[...]]))
    def _():
        q = q_ref[...].astype(F32)
        qs = [q[:, hd * SB_DIM:(hd + 1) * SB_DIM].astype(BF16) for hd in range(SB_HEADS)]
        c_ref[...] = _sb_pages(qs, k_pages, v_pages, ktb, vtb, u_ref[...], c_ref[...], acc_o)


def _sb_dec(page_table, qsb_s, k_new, v_new, u, u_new, cache_k_t, cache_v_t, layer, n_recent, npg_old, n_new):
    db, n_pages = page_table.shape
    rows = SB_HEADS * DEC_ROWS
    page = (None, None, SB_HEADS, SB_DIM, LANES)
    seq = lambda w: pl.BlockSpec((None, DEC_ROWS, w), lambda bi, c, pt: (bi, 0, 0))
    per_seq = lambda w: pl.BlockSpec((None, rows, w), lambda bi, c, pt: (bi, 0, 0))
    const = lambda shape: pl.BlockSpec(shape, lambda bi, c, pt: (0,) * len(shape))
    tb = lambda n: [pltpu.VMEM((SB_HEADS, SB_DIM, n * LANES), BF16)] * 2
    n_old = n_pages - n_recent

    gs = pltpu.PrefetchScalarGridSpec(
        num_scalar_prefetch=1, grid=(db, 1),
        in_specs=([seq(384), seq(384), seq(384), const(u.shape), const(u_new.shape)]
                  + _page_specs(page, layer, n_old, n_recent, False, 1) * 2),
        out_specs=[per_seq(SB_DIM), per_seq(1)],
        scratch_shapes=tb(n_recent))
    acc, c = pl.pallas_call(
        functools.partial(_sb_dec_new_body, npg=n_recent, n_new=n_new),
        grid_spec=gs,
        out_shape=[jax.ShapeDtypeStruct((db, rows, SB_DIM), F32), jax.ShapeDtypeStruct((db, rows, 1), F32)],
        compiler_params=_cparams(("parallel", "arbitrary")),
        name="sb_dec_new",
    )(page_table, qsb_s, k_new, v_new, u, u_new, *([cache_k_t] * n_recent), *([cache_v_t] * n_recent))
    if n_old == 0:
        return acc

    n_chunks = n_old // npg_old
    gs_old = pltpu.PrefetchScalarGridSpec(
        num_scalar_prefetch=1, grid=(db, n_chunks),
        in_specs=([seq(384), per_seq(SB_DIM), per_seq(1), const(u.shape)]
                  + _page_specs(page, layer, 0, npg_old, True, n_chunks) * 2),
        out_specs=per_seq(SB_DIM),
        scratch_shapes=tb(npg_old) + [pltpu.VMEM((rows, 1), F32)])

    def older():
        return pl.pallas_call(
            functools.partial(_sb_dec_old_body, npg=npg_old),
            grid_spec=gs_old,
            out_shape=jax.ShapeDtypeStruct((db, rows, SB_DIM), F32),
            compiler_params=_cparams(("parallel", "arbitrary")),
            name="sb_dec_old",
        )(page_table, qsb_s, acc, c, u, *([cache_k_t] * npg_old), *([cache_v_t] * npg_old))

    return lax.cond(jnp.max(c) > SB_DEAD, older, lambda: acc)


def _df_q_offset(hd, c):
    return (hd % 2) * LANES + (hd // 2) * 64 + c * DF_D


def _df_dec_body(pt_ref, q_ref, knew_ref, vnew_ref, lq1_ref, lk1_ref, lq2_ref, lk2_ref, g_ref, *rest,
                 npg, gp, n_chunks, n_new, lam_init):
    k_pages = rest[:npg]
    v_pages = rest[npg:2 * npg]
    o_ref = rest[2 * npg]
    ktb, vtb, qs, m_ref, l_ref, acc_ref = rest[2 * npg + 1:]
    c = pl.program_id(1)
    grp = 2 * DEC_ROWS
    rows = 2 * DF_KV_HEADS * grp

    @pl.when(c == 0)
    def _():
        q = q_ref[...].astype(F32)
        pieces = []
        for g in range(DF_KV_HEADS):
            for cc in range(2):
                for h2 in range(2):
                    off = _df_q_offset(2 * g + h2, cc)
                    pieces.append(q[:, off:off + DF_D])
        qs[...] = jnp.concatenate(pieces, axis=0).astype(BF16)
        m_ref[...] = jnp.full(m_ref.shape, NEG_BIG, F32)
        l_ref[...] = jnp.zeros(l_ref.shape, F32)
        acc_ref[...] = jnp.zeros(acc_ref.shape, F32)

    def gc_rows(gc):
        return slice(gc * grp, (gc + 1) * grp)

    def kv_rows(g):
        return slice(g * 2 * grp, (g + 1) * 2 * grp)

    parts = []
    for g0 in range(0, npg, gp):
        for i in range(g0, g0 + gp):
            for g in range(DF_KV_HEADS):
                vtb[g, :, i * LANES:(i + 1) * LANES] = v_pages[i][g].astype(BF16)
                for cc in range(2):
                    ktb[2 * g + cc, :, i * LANES:(i + 1) * LANES] = k_pages[i][g, cc].astype(BF16)
        span = slice(g0 * LANES, (g0 + gp) * LANES)
        s = jnp.concatenate([_dot(qs[gc_rows(gc), :], ktb[gc, :, span]) for gc in range(2 * DF_KV_HEADS)], axis=0)

        def pv_pages(p, span=span):
            return jnp.concatenate([_dot_t(p[kv_rows(g), :], vtb[g, :, span]) for g in range(DF_KV_HEADS)], axis=0)

        parts.append(_softmax_partial(s, pv_pages))
    _merge_partials(parts, m_ref, l_ref, acc_ref)

    @pl.when(c == n_chunks - 1)
    def _():
        kn = knew_ref[...].astype(BF16)
        vn = vnew_ref[...].astype(BF16)
        s_new = jnp.concatenate([_dot_t(qs[gc_rows(gc), :], kn[:, gc * DF_D:(gc + 1) * DF_D])
                                 for gc in range(2 * DF_KV_HEADS)], axis=0)
        s_new = jnp.where(_new_key_mask(rows, n_new, False), s_new, -jnp.inf)

        def pv_new(p):
            return jnp.concatenate([_dot(p[kv_rows(g), :], vn[:, g * DF_V:(g + 1) * DF_V])
                                    for g in range(DF_KV_HEADS)], axis=0)

        _merge_partials([_softmax_partial(s_new, pv_new)], m_ref, l_ref, acc_ref)
        o = acc_ref[...] / l_ref[...]
        lam = _lam(lq1_ref[...], lk1_ref[...], lq2_ref[...], lk2_ref[...], lam_init)
        for g in range(DF_KV_HEADS):
            d = o[gc_rows(2 * g), :] - lam * o[gc_rows(2 * g + 1), :]
            y = _rms(d, g_ref[...]) * (1.0 - lam_init)
            for h2 in range(2):
                o_ref[2 * g + h2] = y[_head_rows(h2), :]


def _df_dec(page_table, qdf_s, k_new, v_new, lams, g_df, cache_k_t, cache_v_t, layer, npg, gp, n_new, lam_init):
    db, n_pages = page_table.shape
    n_chunks = n_pages // npg
    rows = 4 * DF_KV_HEADS * DEC_ROWS
    seq = lambda w: pl.BlockSpec((None, DEC_ROWS, w), lambda bi, c, pt: (bi, 0, 0))
    const = lambda shape: pl.BlockSpec(shape, lambda bi, c, pt: (0,) * len(shape))
    in_specs = ([seq(256), seq(LANES), seq(LANES)] + [const((1, DF_D))] * 4 + [const((1, DF_V))]
                + _page_specs((None, None, DF_KV_HEADS, 2, DF_D, LANES), layer, 0, npg, False, n_chunks)
                + _page_specs((None, None, DF_KV_HEADS, DF_V, LANES), layer, 0, npg, False, n_chunks))
    gs = pltpu.PrefetchScalarGridSpec(
        num_scalar_prefetch=1, grid=(db, n_chunks), in_specs=in_specs,
        out_specs=pl.BlockSpec((None, DF_HEADS, DEC_ROWS, DF_V), lambda bi, c, pt: (bi, 0, 0, 0)),
        scratch_shapes=[pltpu.VMEM((2 * DF_KV_HEADS, DF_D, npg * LANES), BF16),
                        pltpu.VMEM((DF_KV_HEADS, DF_V, npg * LANES), BF16),
                        pltpu.VMEM((rows, DF_D), BF16), pltpu.VMEM((rows, 1), F32), pltpu.VMEM((rows, 1), F32),
                        pltpu.VMEM((rows, DF_V), F32)])
    return pl.pallas_call(
        functools.partial(_df_dec_body, npg=npg, gp=gp, n_chunks=n_chunks, n_new=n_new, lam_init=lam_init),
        grid_spec=gs,
        out_shape=jax.ShapeDtypeStruct((db, DF_HEADS, DEC_ROWS, DF_V), F32),
        compiler_params=_cparams(("parallel", "arbitrary")),
        name="df_dec",
    )(page_table, qdf_s, k_new, v_new, *lams, g_df, *([cache_k_t] * npg), *([cache_v_t] * npg))


def _sample_rows(a, db, dt):
    return jnp.pad(a.reshape(db, dt, a.shape[-1]), ((0, 0), (0, DEC_ROWS - dt), (0, 0)))


def _heads_to_rows(o, dt, order=None):
    if order is not None:
        o = o[:, jnp.array(order)]
    db, h, _, w = o.shape
    return jnp.transpose(o[:, :, :dt], (0, 2, 1, 3)).reshape(db * dt, h * w)


_DF_HEAD_ORDER = (0, 2, 1, 3)


def kernel(x_prompt, x_sample, cache_mla_ckv, cache_mla_krope, cache_sb_k, cache_sb_v, cache_df_k, cache_df_v,
           page_table, g_ffn1, w1_gate, w1_up, w1_down, g_mix, w_in, g_q_lat, w_uq, g_kv_lat, w_uk, w_uv,
           g_mla_out, g_sb_out, lam_q1, lam_k1, lam_q2, lam_k2, g_df_sub, w_out, g_ffn2, w2_gate, w2_up, w2_down,
           g_final):
    b, t, d = x_prompt.shape
    db, dt, _ = x_sample.shape
    depth = g_ffn1.shape[0]
    n_pages = page_table.shape[1]
    page = cache_mla_ckv.shape[2]
    assert page == LANES and dt <= DEC_ROWS and d == 1024
    bt, ns = b * t, db * dt
    tq = _pick_tile(t, (512, 256, 128))
    tq_sb = _pick_tile(t, (256, 128))
    tk = _pick_tile(t, (1024, 512, 256, 128))
    npg = _pick_tile(n_pages, (64, 32, 16, 8, 4, 2, 1))
    gp = npg
    sb_recent = 4 if n_pages >= 8 else max(1, n_pages // 2)
    sb_old = n_pages - sb_recent
    sb_npg_old = max((c for c in range(1, 21) if sb_old % c == 0), default=1)

    inv_freq = ROPE_THETA ** (-jnp.arange(0, DF_D, 2, dtype=F32) / DF_D)

    def rope_tables(pos, reps):
        ang = jnp.tile(pos, reps).astype(F32)[:, None] * inv_freq
        return jnp.tile(jnp.cos(ang), (1, LANES // 16)), jnp.tile(jnp.sin(ang), (1, LANES // 16))

    rope_p = rope_tables(jnp.arange(t, dtype=jnp.int32), b)
    rope_s = rope_tables(n_pages * page + jnp.arange(dt, dtype=jnp.int32), db)

    cache_kr_t = jnp.transpose(cache_mla_krope, (0, 1, 3, 2))
    cache_sbk_t = jnp.transpose(cache_sb_k, (0, 1, 3, 4, 2))
    cache_sbv_t = jnp.transpose(cache_sb_v, (0, 1, 3, 4, 2))
    cache_dfk_t = jnp.transpose(cache_df_k, (0, 1, 3, 4, 5, 2))
    cache_dfv_t = jnp.transpose(cache_df_v, (0, 1, 3, 4, 2))

    tri = lambda n: jnp.tril(jnp.ones((n, n), F32)).astype(BF16)
    u_q, u_p, u_new = tri(tq_sb), tri(LANES), tri(DEC_ROWS)

    xp = x_prompt.reshape(bt, d)
    xs = x_sample.reshape(ns, d)
    rows_p = [[] for _ in range(6)]
    rows_s = [[] for _ in range(6)]
    row2 = lambda v: v.reshape(1, -1)
    df_perm = jnp.array([0, 2, 1, 3])

    for l in range(depth):
        lam_init = 0.8 - 0.6 * math.exp(-0.3 * l)
        wi = w_in[l]
        q_df_cols = wi[:, 1696:1952].reshape(d, DF_HEADS, 2 * DF_D)[:, df_perm].reshape(d, 256)
        w_in_r = jnp.concatenate([wi[:, :512], jnp.pad(wi[:, 512:544], ((0, 0), (0, LANES - MLA_ROPE))),
                                  wi[:, 544:1696], q_df_cols, wi[:, 1952:2208]], axis=1).astype(BF16)
        wq = w_uq[l].reshape(256, MLA_HEADS, MLA_NOPE + MLA_ROPE)
        wq_blocks = []
        for hd in range(MLA_HEADS):
            blk = jnp.zeros((256, 256), F32)
            blk = blk.at[:, (hd % 2) * 64:(hd % 2) * 64 + MLA_NOPE].set(wq[:, hd, :MLA_NOPE])
            blk = blk.at[:, LANES:LANES + MLA_ROPE].set(wq[:, hd, MLA_NOPE:])
            wq_blocks.append(blk)
        w_uq_r = jnp.concatenate(wq_blocks, axis=1).astype(BF16)
        w_uk2 = w_uk[l].reshape(256, 384).astype(BF16)
        w_uv2 = w_uv[l].reshape(256, 384).astype(BF16)
        w_ukt = jnp.transpose(w_uk2.reshape(256, 3, LANES), (1, 2, 0))
        wo = w_out[l]
        w_out_r = jnp.concatenate([wo[:768], wo[768:].reshape(DF_HEADS, DF_V, d)[df_perm].reshape(256, d)],
                                  axis=0).astype(BF16)
        lams = (row2(lam_q1[l]), row2(lam_k1[l]), row2(lam_q2[l]), row2(lam_k2[l]))
        g_df = row2(g_df_sub[l])
        g_df2 = jnp.tile(g_df, (1, 2))

        ffn1 = functools.partial(_ffn, g=row2(g_ffn1[l]), wg=w1_gate[l].astype(BF16), wu=w1_up[l].astype(BF16),
                                 wd=w1_down[l].astype(BF16), g_final=row2(g_final), final_norm=False)
        ffn2 = functools.partial(_ffn, g=row2(g_ffn2[l]), wg=w2_gate[l].astype(BF16), wu=w2_up[l].astype(BF16),
                                 wd=w2_down[l].astype(BF16), g_final=row2(g_final), final_norm=l == depth - 1)
        inproj = functools.partial(_inproj, g_mix=row2(g_mix[l]), w_in=w_in_r, g_q=row2(g_q_lat[l]), w_uq=w_uq_r,
                                   g_kv=row2(g_kv_lat[l]), w_uk=w_uk2, w_uv=w_uv2)
        outproj = functools.partial(_outproj, g_mla=row2(g_mla_out[l]), g_sb=row2(g_sb_out[l]), w_out=w_out_r)

        xp = ffn1(xp)
        xs = ffn1(xs)
        new_p = inproj(xp, cos=rope_p[0], sin=rope_p[1])
        new_s = inproj(xs, cos=rope_s[0], sin=rope_s[1])

        (_, _, _, _, _, _, qm, km, vm, qsb, ksb_b, vsb_b, qdf, kdf_b, vdf_b) = new_p
        o_mla_p = _mla_prompt(qm, km, vm, b, t, tq, tk)
        o_sb_p = _sb_prompt(qsb, ksb_b, vsb_b, u_q, b, t, tq_sb)
        o_df_p = _df_prompt(qdf, kdf_b, vdf_b, lams, g_df2, b, t, tq, tk, lam_init)

        ckv, krope, ksb, vsb, kdf, vdf, qm, _, _, qsb, _, _, qdf, _, _ = [_sample_rows(a, db, dt) for a in new_s]
        o_mla_s = _mla_dec(page_table, qm, ckv, krope, w_ukt, w_uv2, cache_mla_ckv, cache_kr_t, l, npg, gp, dt)
        o_sb_s = _sb_dec(page_table, qsb, ksb, vsb, u_p, u_new, cache_sbk_t, cache_sbv_t, l, sb_recent,
                         sb_npg_old, dt).reshape(db, SB_HEADS, DEC_ROWS, SB_DIM)
        o_df_s = _df_dec(page_table, qdf, kdf, vdf, lams, g_df, cache_dfk_t, cache_dfv_t, l, npg, gp, dt, lam_init)

        xp = ffn2(outproj(xp, o_mla_p, o_sb_p, o_df_p))
        xs = ffn2(outproj(xs, _heads_to_rows(o_mla_s, dt), _heads_to_rows(o_sb_s, dt),
                          _heads_to_rows(o_df_s, dt, _DF_HEAD_ORDER)))

        for i in range(6):
            rows_p[i].append(new_p[i])
            rows_s[i].append(new_s[i])

    def stack(rows, lead, tail):
        return jnp.stack(rows, axis=0).reshape((depth,) + lead + tail)

    tails = ((256,), (MLA_ROPE,), (SB_HEADS, SB_DIM), (SB_HEADS, SB_DIM), (DF_KV_HEADS, 2, DF_D), (DF_KV_HEADS, DF_V))
    outs = [xp.reshape(b, t, d), xs.reshape(db, dt, d)]
    for i in range(6):
        outs.append(stack(rows_p[i], (b, t), tails[i]))
        outs.append(stack(rows_s[i], (db, dt), tails[i]))
    return tuple(outs)
```

```python
import functools
import math

import jax
import jax.numpy as jnp
from jax import lax
from jax.experimental import pallas as pl
from jax.experimental.pallas import tpu as pltpu

F32 = jnp.float32
BF16 = jnp.bfloat16

EPS = 1e-6
ROPE_THETA = 10000.0
MLA_HEADS, MLA_NOPE, MLA_ROPE, MLA_V = 6, 64, 32, 64
SB_HEADS, SB_DIM = 6, 64
DF_HEADS, DF_KV_HEADS, DF_D, DF_V = 4, 2, 32, 64
LOG2E = math.log2(math.e)
MLA_SCALE = (MLA_NOPE + MLA_ROPE) ** -0.5 * LOG2E
DF_SCALE = DF_D ** -0.5 * LOG2E
SB_SCALE = SB_DIM ** -0.5
SB_DEAD = -104.0

LANES = 128
DEC_ROWS = 16
VMEM_LIMIT = 56 * 1024 * 1024
NEG_BIG = -1e30


def _cparams(sem):
    return pltpu.CompilerParams(dimension_semantics=sem, vmem_limit_bytes=VMEM_LIMIT)


def _rms(x, g):
    return x * lax.rsqrt(jnp.mean(x * x, axis=-1, keepdims=True) + EPS) * g


def _dot(a, b):
    return jnp.dot(a, b, preferred_element_type=F32)


def _dot_t(a, b):
    return lax.dot_general(a, b, (((1,), (1,)), ((), ())), preferred_element_type=F32)


def _pick_tile(n, candidates):
    for c in candidates:
        if n % c == 0:
            return c
    raise ValueError(f"no tile for {n}")


def _full(shape):
    nd = len(shape)
    return pl.BlockSpec(shape, lambda *_: (0,) * nd)


def _ffn_body(x_ref, g_ref, wg_ref, wu_ref, wd_ref, gf_ref, o_ref, *, ff_chunk, final_norm):
    x = x_ref[...]
    h = _rms(x, g_ref[...]).astype(BF16)
    acc = jnp.zeros_like(x)
    d_ff = wg_ref.shape[1]
    for c0 in range(0, d_ff, ff_chunk):
        gate = _dot(h, wg_ref[:, c0:c0 + ff_chunk])
        up = _dot(h, wu_ref[:, c0:c0 + ff_chunk])
        act = (gate * jax.nn.sigmoid(gate) * up).astype(BF16)
        acc = acc + _dot(act, wd_ref[c0:c0 + ff_chunk, :])
    y = x + 0.5 * acc
    if final_norm:
        y = _rms(y, gf_ref[...])
    o_ref[...] = y


def _ffn(x, g, wg, wu, wd, g_final, final_norm):
    n, d = x.shape
    d_ff = wg.shape[1]
    tm = _pick_tile(n, (512, 256, 128, 64, 32, 16, 8))
    ff_chunk = d_ff // 2 if (d_ff // 2) % LANES == 0 else d_ff
    row = pl.BlockSpec((tm, d), lambda i: (i, 0))
    return pl.pallas_call(
        functools.partial(_ffn_body, ff_chunk=ff_chunk, final_norm=final_norm),
        grid=(n // tm,),
        in_specs=[row, _full((1, d)), _full((d, d_ff)), _full((d, d_ff)), _full((d_ff, d)), _full((1, d))],
        out_specs=row,
        out_shape=jax.ShapeDtypeStruct((n, d), F32),
        compiler_params=_cparams(("parallel",)),
        name="ffn",
    )(x, g, wg, wu, wd, g_final)


def _rope128(x, cos, sin):
    lane = lax.broadcasted_iota(jnp.int32, x.shape, 1)
    first = (lane % 32) < 16
    rot = jnp.where(first, -pltpu.roll(x, LANES - 16, 1), pltpu.roll(x, 16, 1))
    return x * cos + rot * sin


_CQ, _CKV, _KR, _QSB, _KSB, _VSB, _QDF, _KDF, _VDF, _PROJ_W = 0, 256, 512, 640, 1024, 1408, 1792, 2048, 2176, 2304


def _inproj_body(x_ref, g_ref, win_ref, gq_ref, wuq_ref, gkv_ref, wuk_ref, wuv_ref, cos_ref, sin_ref,
                 ckv_o, krope_o, ksb_o, vsb_o, kdf_o, vdf_o,
                 qm_o, km_o, vm_o, qsb_o, ksbb_o, vsbb_o, qdf_o, kdfb_o, vdfb_o):
    h = _rms(x_ref[...], g_ref[...]).astype(BF16)
    proj = _dot(h, win_ref[...])
    cos = cos_ref[...]
    sin = sin_ref[...]

    cq = _rms(proj[:, _CQ:_CQ + 256], gq_ref[...]).astype(BF16)
    q = _dot(cq, wuq_ref[...])
    for hd in range(MLA_HEADS):
        base = hd * 256
        qm_o[:, base:base + LANES] = (q[:, base:base + LANES] * MLA_SCALE).astype(BF16)
        qr = _rope128(q[:, base + LANES:base + 2 * LANES], cos, sin)
        qm_o[:, base + LANES:base + 2 * LANES] = (qr * MLA_SCALE).astype(BF16)
    ckv = _rms(proj[:, _CKV:_CKV + 256], gkv_ref[...])
    ckv_o[...] = ckv
    ckv_b = ckv.astype(BF16)
    k_nope = _dot(ckv_b, wuk_ref[...])
    vm_o[...] = _dot(ckv_b, wuv_ref[...]).astype(BF16)
    krope = _rope128(proj[:, _KR:_KR + LANES], cos, sin)
    krope_o[...] = krope[:, :MLA_ROPE]
    krope_b = krope.astype(BF16)
    for p in range(MLA_HEADS // 2):
        km_o[:, p * 256:p * 256 + LANES] = k_nope[:, p * LANES:(p + 1) * LANES].astype(BF16)
        km_o[:, p * 256 + LANES:(p + 1) * 256] = krope_b

    qsb_o[...] = (proj[:, _QSB:_QSB + 384] * SB_SCALE).astype(BF16)
    ksb = proj[:, _KSB:_KSB + 384]
    vsb = proj[:, _VSB:_VSB + 384]
    ksb_o[...] = ksb
    vsb_o[...] = vsb
    ksbb_o[...] = ksb.astype(BF16)
    vsbb_o[...] = vsb.astype(BF16)

    for c in range(2):
        qd = _rope128(proj[:, _QDF + c * LANES:_QDF + (c + 1) * LANES], cos, sin)
        qdf_o[:, c * LANES:(c + 1) * LANES] = (qd * DF_SCALE).astype(BF16)
    kdf = _rope128(proj[:, _KDF:_KDF + LANES], cos, sin)
    vdf = proj[:, _VDF:_VDF + LANES]
    kdf_o[...] = kdf
    vdf_o[...] = vdf
    kdfb_o[...] = kdf.astype(BF16)
    vdfb_o[...] = vdf.astype(BF16)


def _inproj(x, g_mix, w_in, g_q, w_uq, g_kv, w_uk, w_uv, cos, sin):
    n, d = x.shape
    tm = _pick_tile(n, (512, 256, 128, 64, 32, 16, 8))
    row = lambda w: pl.BlockSpec((tm, w), lambda i: (i, 0))
    widths_f32 = (256, MLA_ROPE, 384, 384, LANES, LANES)
    widths_bf16 = (MLA_HEADS * 256, 3 * 256, 384, 384, 384, 384, 256, LANES, LANES)
    out_shape = ([jax.ShapeDtypeStruct((n, w), F32) for w in widths_f32]
                 + [jax.ShapeDtypeStruct((n, w), BF16) for w in widths_bf16])
    return pl.pallas_call(
        _inproj_body,
        grid=(n // tm,),
        in_specs=[row(d), _full((1, d)), _full(w_in.shape), _full((1, 256)), _full(w_uq.shape), _full((1, 256)),
                  _full(w_uk.shape), _full(w_uv.shape), row(LANES), row(LANES)],
        out_specs=[row(w) for w in widths_f32 + widths_bf16],
        out_shape=out_shape,
        compiler_params=_cparams(("parallel",)),
        name="inproj",
    )(x, g_mix, w_in, g_q, w_uq, g_kv, w_uk, w_uv, cos, sin)


def _outproj_body(x_ref, om_ref, os_ref, od_ref, gm_ref, gs_ref, w_ref, o_ref):
    mixed_m = _rms(om_ref[...], gm_ref[...]).astype(BF16)
    mixed_s = _rms(os_ref[...], gs_ref[...]).astype(BF16)
    mixed_d = od_ref[...].astype(BF16)
    y = _dot(mixed_m, w_ref[0:384, :]) + _dot(mixed_s, w_ref[384:768, :]) + _dot(mixed_d, w_ref[768:1024, :])
    o_ref[...] = x_ref[...] + y


def _outproj(x, o_mla, o_sb, o_df, g_mla, g_sb, w_out):
    n, d = x.shape
    tm = _pick_tile(n, (512, 256, 128, 64, 32, 16, 8))
    row = lambda w: pl.BlockSpec((tm, w), lambda i: (i, 0))
    return pl.pallas_call(
        _outproj_body,
        grid=(n // tm,),
        in_specs=[row(d), row(384), row(384), row(256), _full((1, 384)), _full((1, 384)), _full(w_out.shape)],
        out_specs=row(d),
        out_shape=jax.ShapeDtypeStruct((n, d), F32),
        compiler_params=_cparams(("parallel",)),
        name="outproj",
    )(x, o_mla, o_sb, o_df, g_mla, g_sb, w_out)


def _softmax_attend(qs, k_ref, v_ref, qi, tq, tk):
    def step(off, carry, masked, width=tk):
        k = k_ref[pl.ds(off, width), :]
        v = v_ref[pl.ds(off, width), :]
        if masked:
            row = lax.broadcasted_iota(jnp.int32, (tq, width), 0) + qi * tq
            col = lax.broadcasted_iota(jnp.int32, (tq, width), 1) + off
            valid = col <= row
        out = []
        for q, (m, l, acc) in zip(qs, carry):
            s = _dot_t(q, k)
            if masked:
                s = jnp.where(valid, s, -jnp.inf)
            m_new = jnp.maximum(m, jnp.max(s, axis=-1, keepdims=True))
            p = jnp.exp2(s - m_new)
            alpha = jnp.exp2(m - m_new)
            l = alpha * l + jnp.sum(p, axis=-1, keepdims=True)
            acc = alpha * acc + _dot(p.astype(BF16), v)
            out.append((m_new, l, acc))
        return tuple(out)

    init = tuple((jnp.full((tq, 1), NEG_BIG, F32), jnp.zeros((tq, 1), F32), jnp.zeros((tq, LANES), F32)) for _ in qs)
    n_full = (qi * tq) // tk
    carry = lax.fori_loop(0, n_full, lambda j, c: step(pl.multiple_of(j * tk, tk), c, False), init)
    off = pl.multiple_of(n_full * tk, tk)
    if tk == 2 * tq:
        carry = lax.cond(qi * tq == n_full * tk,
                         lambda c: step(off, c, True, tq), lambda c: step(off, c, True), carry)
    else:
        carry = step(off, carry, True)
    return [(acc, l) for _, l, acc in carry]


def _mla_prompt_body(q_ref, k_ref, v_ref, o_ref, *, tq, tk):
    qi = pl.program_id(2)
    lane = lax.broadcasted_iota(jnp.int32, (tq, LANES), 1)
    (acc0, l0), (acc1, l1) = _softmax_attend([q_ref[:, 0:256], q_ref[:, 256:512]], k_ref, v_ref, qi, tq, tk)
    o_ref[...] = jnp.where(lane < 64, acc0 / l0, acc1 / l1)


def _mla_prompt(qm, km, vm, b, t, tq, tk):
    nq = t // tq
    return pl.pallas_call(
        functools.partial(_mla_prompt_body, tq=tq, tk=tk),
        grid=(b, MLA_HEADS // 2, nq),
        in_specs=[pl.BlockSpec((tq, 512), lambda bi, p, qi: (bi * nq + qi, p)),
                  pl.BlockSpec((t, 256), lambda bi, p, qi: (bi, p)),
                  pl.BlockSpec((t, LANES), lambda bi, p, qi: (bi, p))],
        out_specs=pl.BlockSpec((tq, LANES), lambda bi, p, qi: (bi * nq + qi, p)),
        out_shape=jax.ShapeDtypeStruct((b * t, 384), F32),
        compiler_params=_cparams(("parallel", "parallel", "arbitrary")),
        name="mla_prompt",
    )(qm, km, vm)


def _log_keep(z):
    return jnp.minimum(-z, 0.0) - jnp.log1p(jnp.exp(-jnp.abs(z)))


def _suffix_sum(lk, u):
    hi = lk.astype(BF16)
    lo = (lk - hi.astype(F32)).astype(BF16)
    return _dot(hi, u) + _dot(lo, u)


def _sb_weights(z, mask, c, u):
    lk = _log_keep(z)
    if mask is not None:
        lk = jnp.where(mask, lk, 0.0)
    incl = _suffix_sum(lk, u)
    a = jnp.exp(z + incl + c)
    if mask is not None:
        a = jnp.where(mask, a, 0.0)
    return a, c + incl[:, 0:1]


def _any_alive(cs):
    return functools.reduce(jnp.maximum, [jnp.max(c) for c in cs]) > SB_DEAD


def _sb_attend(qs, k_ref, v_ref, u, qi, tq):
    def step(off, carry, mask):
        out = []
        for (q, col0), (c, acc) in zip(qs, carry):
            k = k_ref[pl.ds(off, tq), col0:col0 + LANES]
            v = v_ref[pl.ds(off, tq), col0:col0 + LANES]
            a, c = _sb_weights(_dot_t(q, k), mask, c, u)
            out.append((c, acc + _dot(a.astype(BF16), v)))
        return tuple(out)

    row = lax.broadcasted_iota(jnp.int32, (tq, tq), 0)
    col = lax.broadcasted_iota(jnp.int32, (tq, tq), 1)
    init = tuple((jnp.zeros((tq, 1), F32), jnp.zeros((tq, LANES), F32)) for _ in qs)
    carry = step(pl.multiple_of(qi * tq, tq), init, col < row)

    def cond(state):
        i, carry = state
        return jnp.logical_and(i < qi, _any_alive([c for c, _ in carry]))

    def body(state):
        i, carry = state
        return i + 1, step(pl.multiple_of((qi - 1 - i) * tq, tq), carry, None)

    _, carry = lax.while_loop(cond, body, (jnp.int32(0), carry))
    return [acc for _, acc in carry]


def _sb_prompt_body(q_ref, k_ref, v_ref, u_ref, o_ref, *, tq):
    qi = pl.program_id(1)
    lane = lax.broadcasted_iota(jnp.int32, (tq, LANES), 1)
    qs = []
    for p in range(SB_HEADS // 2):
        q = q_ref[:, p * LANES:(p + 1) * LANES].astype(F32)
        qs += [(jnp.where((lane // 64) == hh, q, 0.0).astype(BF16), p * LANES) for hh in range(2)]
    acc = _sb_attend(qs, k_ref, v_ref, u_ref[...], qi, tq)
    for p in range(SB_HEADS // 2):
        o_ref[:, p * LANES:(p + 1) * LANES] = jnp.where(lane < 64, acc[2 * p], acc[2 * p + 1])


def _sb_prompt(qsb, ksb, vsb, u, b, t, tq):
    nq = t // tq
    width = SB_HEADS * SB_DIM
    return pl.pallas_call(
        functools.partial(_sb_prompt_body, tq=tq),
        grid=(b, nq),
        in_specs=[pl.BlockSpec((tq, width), lambda bi, qi: (bi * nq + qi, 0)),
                  pl.BlockSpec((t, width), lambda bi, qi: (bi, 0)),
                  pl.BlockSpec((t, width), lambda bi, qi: (bi, 0)),
                  _full((tq, tq))],
        out_specs=pl.BlockSpec((tq, width), lambda bi, qi: (bi * nq + qi, 0)),
        out_shape=jax.ShapeDtypeStruct((b * t, width), F32),
        compiler_params=_cparams(("parallel", "arbitrary")),
        name="sb_prompt",
    )(qsb, ksb, vsb, u)


def _lam(lq1, lk1, lq2, lk2, lam_init):
    return (jnp.exp(jnp.sum(lq1 * lk1, axis=-1, keepdims=True))
            - jnp.exp(jnp.sum(lq2 * lk2, axis=-1, keepdims=True)) + lam_init)


def _df_prompt_body(q_ref, k_ref, v_ref, lq1_ref, lk1_ref, lq2_ref, lk2_ref, g_ref, o_ref, *, tq, tk, lam_init):
    qi = pl.program_id(2)
    lane = lax.broadcasted_iota(jnp.int32, (tq, LANES), 1)
    q = q_ref[...].astype(F32)
    lam = _lam(lq1_ref[...], lk1_ref[...], lq2_ref[...], lk2_ref[...], lam_init)
    qs = [jnp.where((lane // DF_D) == gc, q, 0.0).astype(BF16) for gc in range(2 * DF_KV_HEADS)]
    res = _softmax_attend(qs, k_ref, v_ref, qi, tq, tk)
    out = jnp.zeros((tq, LANES), F32)
    for g in range(DF_KV_HEADS):
        (acc1, l1), (acc2, l2) = res[2 * g], res[2 * g + 1]
        own = (lane // 64) == g
        d = jnp.where(own, acc1 / l1 - lam * (acc2 / l2), 0.0)
        ms = jnp.sum(d * d, axis=-1, keepdims=True) * (1.0 / DF_V)
        y = d * lax.rsqrt(ms + EPS) * g_ref[...] * (1.0 - lam_init)
        out = jnp.where(own, y, out)
    o_ref[...] = out


def _df_prompt(qdf, kdf, vdf, lams, g_df2, b, t, tq, tk, lam_init):
    nq = t // tq
    return pl.pallas_call(
        functools.partial(_df_prompt_body, tq=tq, tk=tk, lam_init=lam_init),
        grid=(b, 2, nq),
        in_specs=[pl.BlockSpec((tq, LANES), lambda bi, r, qi: (bi * nq + qi, r)),
                  pl.BlockSpec((t, LANES), lambda bi, r, qi: (bi, 0)),
                  pl.BlockSpec((t, LANES), lambda bi, r, qi: (bi, 0)),
                  _full((1, DF_D)), _full((1, DF_D)), _full((1, DF_D)), _full((1, DF_D)), _full((1, LANES))],
        out_specs=pl.BlockSpec((tq, LANES), lambda bi, r, qi: (bi * nq + qi, r)),
        out_shape=jax.ShapeDtypeStruct((b * t, 256), F32),
        compiler_params=_cparams(("parallel", "parallel", "arbitrary")),
        name="df_prompt",
    )(qdf, kdf, vdf, *lams, g_df2)


def _page_specs(block, layer, first_page, npg, reverse_chunks, n_chunks):
    nz = len(block) - 2

    def imap(bi, c, pt, *, slot):
        chunk = (n_chunks - 1 - c) if reverse_chunks else c
        return (layer, pt[bi, first_page + chunk * npg + slot]) + (0,) * nz

    return [pl.BlockSpec(block, functools.partial(imap, slot=s)) for s in range(npg)]


def _softmax_partial(s, pv):
    m = jnp.max(s, axis=-1, keepdims=True)
    p = jnp.exp2(s - m)
    return m, jnp.sum(p, axis=-1, keepdims=True), pv(p.astype(BF16))


def _merge_partials(parts, m_ref, l_ref, acc_ref):
    m_old = m_ref[...]
    m_new = functools.reduce(jnp.maximum, [m for m, _, _ in parts], m_old)
    w_old = jnp.exp2(m_old - m_new)
    l = w_old * l_ref[...]
    acc = w_old * acc_ref[...]
    for m, lp, ap in parts:
        w = jnp.exp2(m - m_new)
        l = l + w * lp
        acc = acc + w * ap
    m_ref[...] = m_new
    l_ref[...] = l
    acc_ref[...] = acc


def _new_key_mask(rows, n_new, strict):
    t = lax.broadcasted_iota(jnp.int32, (rows, DEC_ROWS), 0) % DEC_ROWS
    kk = lax.broadcasted_iota(jnp.int32, (rows, DEC_ROWS), 1)
    return ((kk < t) if strict else (kk <= t)) & (kk < n_new)


def _head_rows(hd):
    return slice(hd * DEC_ROWS, (hd + 1) * DEC_ROWS)


def _mla_dec_body(pt_ref, q_ref, cnew_ref, rnew_ref, wukt_ref, wuv_ref, *rest, npg, gp, n_chunks, n_new):
    ckv_pages = rest[:npg]
    kr_pages = rest[npg:2 * npg]
    o_ref = rest[2 * npg]
    kvb, krb, qlat, qrope, m_ref, l_ref, acc_ref = rest[2 * npg + 1:]
    c = pl.program_id(1)
    rows = MLA_HEADS * DEC_ROWS

    @pl.when(c == 0)
    def _():
        for hd in range(MLA_HEADS):
            base = hd * 256
            ql = _dot(q_ref[:, base:base + LANES], wukt_ref[hd // 2])
            qlat[_head_rows(hd), :] = ql.astype(BF16)
            qrope[_head_rows(hd), :] = q_ref[:, base + LANES:base + 2 * LANES]
        m_ref[...] = jnp.full(m_ref.shape, NEG_BIG, F32)
        l_ref[...] = jnp.zeros(l_ref.shape, F32)
        acc_ref[...] = jnp.zeros(acc_ref.shape, F32)

    ql = qlat[...]
    qr = qrope[:, :MLA_ROPE]
    parts = []
    for g0 in range(0, npg, gp):
        for i in range(g0, g0 + gp):
            kvb[i * LANES:(i + 1) * LANES, :] = ckv_pages[i][...].astype(BF16)
            krb[:, i * LANES:(i + 1) * LANES] = kr_pages[i][...].astype(BF16)
        span = slice(g0 * LANES, (g0 + gp) * LANES)
        s = _dot_t(ql, kvb[span, :]) + _dot(qr, krb[:, span])
        parts.append(_softmax_partial(s, lambda p, span=span: _dot(p, kvb[span, :])))
    _merge_partials(parts, m_ref, l_ref, acc_ref)

    @pl.when(c == n_chunks - 1)
    def _():
        cn = cnew_ref[...].astype(BF16)
        rn = rnew_ref[...].astype(BF16)
        s_new = _dot_t(ql, cn) + _dot_t(qr, rn)
        s_new = jnp.where(_new_key_mask(rows, n_new, False), s_new, -jnp.inf)
        _merge_partials([_softmax_partial(s_new, lambda p: _dot(p, cn))], m_ref, l_ref, acc_ref)
        o_lat = (acc_ref[...] / l_ref[...]).astype(BF16)
        for hd in range(MLA_HEADS):
            o_ref[hd] = _dot(o_lat[_head_rows(hd), :], wuv_ref[:, hd * MLA_V:(hd + 1) * MLA_V])


def _mla_dec(page_table, qm_s, ckv_new, kr_new, w_ukt, w_uv, cache_ckv, cache_kr_t, layer, npg, gp, n_new):
    db, n_pages = page_table.shape
    n_chunks = n_pages // npg
    rows = MLA_HEADS * DEC_ROWS
    seq = lambda w: pl.BlockSpec((None, DEC_ROWS, w), lambda bi, c, pt: (bi, 0, 0))
    const = lambda shape: pl.BlockSpec(shape, lambda bi, c, pt: (0,) * len(shape))
    in_specs = ([seq(MLA_HEADS * 256), seq(256), seq(MLA_ROPE), const(w_ukt.shape), const(w_uv.shape)]
                + _page_specs((None, None, LANES, 256), layer, 0, npg, False, n_chunks)
                + _page_specs((None, None, MLA_ROPE, LANES), layer, 0, npg, False, n_chunks))
    gs = pltpu.PrefetchScalarGridSpec(
        num_scalar_prefetch=1, grid=(db, n_chunks), in_specs=in_specs,
        out_specs=pl.BlockSpec((None, MLA_HEADS, DEC_ROWS, MLA_V), lambda bi, c, pt: (bi, 0, 0, 0)),
        scratch_shapes=[pltpu.VMEM((npg * LANES, 256), BF16), pltpu.VMEM((MLA_ROPE, npg * LANES), BF16),
                        pltpu.VMEM((rows, 256), BF16), pltpu.VMEM((rows, LANES), BF16),
                        pltpu.VMEM((rows, 1), F32), pltpu.VMEM((rows, 1), F32), pltpu.VMEM((rows, 256), F32)])
    return pl.pallas_call(
        functools.partial(_mla_dec_body, npg=npg, gp=gp, n_chunks=n_chunks, n_new=n_new),
        grid_spec=gs,
        out_shape=jax.ShapeDtypeStruct((db, MLA_HEADS, DEC_ROWS, MLA_V), F32),
        compiler_params=_cparams(("parallel", "arbitrary")),
        name="mla_dec",
    )(page_table, qm_s, ckv_new, kr_new, w_ukt, w_uv, *([cache_ckv] * npg), *([cache_kr_t] * npg))


def _sb_pages(qs, k_pages, v_pages, ktb, vtb, u, c, acc_ref):
    npg = len(k_pages)
    for i in range(npg):
        for hd in range(SB_HEADS):
            ktb[hd, :, i * LANES:(i + 1) * LANES] = k_pages[i][hd].astype(BF16)
            vtb[hd, :, i * LANES:(i + 1) * LANES] = v_pages[i][hd].astype(BF16)
    z = jnp.concatenate([_dot(qs[hd], ktb[hd]) for hd in range(SB_HEADS)], axis=0)
    a = [None] * npg
    for blk in range(npg - 1, -1, -1):
        a[blk], c = _sb_weights(z[:, blk * LANES:(blk + 1) * LANES], None, c, u)
    a = jnp.concatenate(a, axis=1).astype(BF16)
    for hd in range(SB_HEADS):
        acc_ref[_head_rows(hd), :] += _dot_t(a[_head_rows(hd), :], vtb[hd])
    return c


def _sb_dec_new_body(pt_ref, q_ref, knew_ref, vnew_ref, u_ref, un_ref, *rest, npg, n_new):
    k_pages = rest[:npg]
    v_pages = rest[npg:2 * npg]
    acc_o, c_o, ktb, vtb = rest[2 * npg:]
    rows = SB_HEADS * DEC_ROWS
    q = q_ref[...].astype(F32)
    kn = knew_ref[...]
    vn = vnew_ref[...]
    head = lambda x, hd: x[:, hd * SB_DIM:(hd + 1) * SB_DIM].astype(BF16)
    qs = [head(q, hd) for hd in range(SB_HEADS)]
    z = jnp.concatenate([_dot_t(qs[hd], head(kn, hd)) for hd in range(SB_HEADS)], axis=0)
    a, c = _sb_weights(z, _new_key_mask(rows, n_new, True), jnp.zeros((rows, 1), F32), un_ref[...])
    a = a.astype(BF16)
    for hd in range(SB_HEADS):
        acc_o[_head_rows(hd), :] = _dot(a[_head_rows(hd), :], head(vn, hd))
    c_o[...] = _sb_pages(qs, k_pages, v_pages, ktb, vtb, u_ref[...], c, acc_o)


def _sb_dec_old_body(pt_ref, q_ref, accin_ref, cin_ref, u_ref, *rest, npg):
    k_pages = rest[:npg]
    v_pages = rest[npg:2 * npg]
    acc_o, ktb, vtb, c_ref = rest[2 * npg:]
    ci = pl.program_id(1)

    @pl.when(ci == 0)
    def _():
        acc_o[...] = accin_ref[...]
        c_ref[...] = cin_ref[...]

    @pl.when(_any_alive([c_ref[...]]))
    def _():
        q = q_ref[...].astype(F32)
        qs = [q[:, hd * SB_DIM:(hd + 1) * SB_DIM].astype(BF16) for hd in range(SB_HEADS)]
        c_ref[...] = _sb_pages(qs, k_pages, v_pages, ktb, vtb, u_ref[...], c_ref[...], acc_o)


def _sb_dec(page_table, qsb_s, k_new, v_new, u, u_new, cache_k_t, cache_v_t, layer, n_recent, npg_old, n_new):
    db, n_pages = page_table.shape
    rows = SB_HEADS * DEC_ROWS
    page = (None, None, SB_HEADS, SB_DIM, LANES)
    seq = lambda w: pl.BlockSpec((None, DEC_ROWS, w), lambda bi, c, pt: (bi, 0, 0))
    per_seq = lambda w: pl.BlockSpec((None, rows, w), lambda bi, c, pt: (bi, 0, 0))
    const = lambda shape: pl.BlockSpec(shape, lambda bi, c, pt: (0,) * len(shape))
    tb = lambda n: [pltpu.VMEM((SB_HEADS, SB_DIM, n * LANES), BF16)] * 2
    n_old = n_pages - n_recent

    gs = pltpu.PrefetchScalarGridSpec(
        num_scalar_prefetch=1, grid=(db, 1),
        in_specs=([seq(384), seq(384), seq(384), const(u.shape), const(u_new.shape)]
                  + _page_specs(page, layer, n_old, n_recent, False, 1) * 2),
        out_specs=[per_seq(SB_DIM), per_seq(1)],
        scratch_shapes=tb(n_recent))
    acc, c = pl.pallas_call(
        functools.partial(_sb_dec_new_body, npg=n_recent, n_new=n_new),
        grid_spec=gs,
        out_shape=[jax.ShapeDtypeStruct((db, rows, SB_DIM), F32), jax.ShapeDtypeStruct((db, rows, 1), F32)],
        compiler_params=_cparams(("parallel", "arbitrary")),
        name="sb_dec_new",
    )(page_table, qsb_s, k_new, v_new, u, u_new, *([cache_k_t] * n_recent), *([cache_v_t] * n_recent))
    if n_old == 0:
        return acc

    n_chunks = n_old // npg_old
    gs_old = pltpu.PrefetchScalarGridSpec(
        num_scalar_prefetch=1, grid=(db, n_chunks),
        in_specs=([seq(384), per_seq(SB_DIM), per_seq(1), const(u.shape)]
                  + _page_specs(page, layer, 0, npg_old, True, n_chunks) * 2),
        out_specs=per_seq(SB_DIM),
        scratch_shapes=tb(npg_old) + [pltpu.VMEM((rows, 1), F32)])

    def older():
        return pl.pallas_call(
            functools.partial(_sb_dec_old_body, npg=npg_old),
            grid_spec=gs_old,
            out_shape=jax.ShapeDtypeStruct((db, rows, SB_DIM), F32),
            compiler_params=_cparams(("parallel", "arbitrary")),
            name="sb_dec_old",
        )(page_table, qsb_s, acc, c, u, *([cache_k_t] * npg_old), *([cache_v_t] * npg_old))

    return lax.cond(jnp.max(c) > SB_DEAD, older, lambda: acc)


def _df_q_offset(hd, c):
    return (hd % 2) * LANES + (hd // 2) * 64 + c * DF_D


def _df_dec_body(pt_ref, q_ref, knew_ref, vnew_ref, lq1_ref, lk1_ref, lq2_ref, lk2_ref, g_ref, *rest,
                 npg, gp, n_chunks, n_new, lam_init):
    k_pages = rest[:npg]
    v_pages = rest[npg:2 * npg]
    o_ref = rest[2 * npg]
    ktb, vtb, qs, m_ref, l_ref, acc_ref = rest[2 * npg + 1:]
    c = pl.program_id(1)
    grp = 2 * DEC_ROWS
    rows = 2 * DF_KV_HEADS * grp

    @pl.when(c == 0)
    def _():
        q = q_ref[...].astype(F32)
        pieces = []
        for g in range(DF_KV_HEADS):
            for cc in range(2):
                for h2 in range(2):
                    off = _df_q_offset(2 * g + h2, cc)
                    pieces.append(q[:, off:off + DF_D])
        qs[...] = jnp.concatenate(pieces, axis=0).astype(BF16)
        m_ref[...] = jnp.full(m_ref.shape, NEG_BIG, F32)
        l_ref[...] = jnp.zeros(l_ref.shape, F32)
        acc_ref[...] = jnp.zeros(acc_ref.shape, F32)

    def gc_rows(gc):
        return slice(gc * grp, (gc + 1) * grp)

    def kv_rows(g):
        return slice(g * 2 * grp, (g + 1) * 2 * grp)

    parts = []
    for g0 in range(0, npg, gp):
        for i in range(g0, g0 + gp):
            for g in range(DF_KV_HEADS):
                vtb[g, :, i * LANES:(i + 1) * LANES] = v_pages[i][g].astype(BF16)
                for cc in range(2):
                    ktb[2 * g + cc, :, i * LANES:(i + 1) * LANES] = k_pages[i][g, cc].astype(BF16)
        span = slice(g0 * LANES, (g0 + gp) * LANES)
        s = jnp.concatenate([_dot(qs[gc_rows(gc), :], ktb[gc, :, span]) for gc in range(2 * DF_KV_HEADS)], axis=0)

        def pv_pages(p, span=span):
            return jnp.concatenate([_dot_t(p[kv_rows(g), :], vtb[g, :, span]) for g in range(DF_KV_HEADS)], axis=0)

        parts.append(_softmax_partial(s, pv_pages))
    _merge_partials(parts, m_ref, l_ref, acc_ref)

    @pl.when(c == n_chunks - 1)
    def _():
        kn = knew_ref[...].astype(BF16)
        vn = vnew_ref[...].astype(BF16)
        s_new = jnp.concatenate([_dot_t(qs[gc_rows(gc), :], kn[:, gc * DF_D:(gc + 1) * DF_D])
                                 for gc in range(2 * DF_KV_HEADS)], axis=0)
        s_new = jnp.where(_new_key_mask(rows, n_new, False), s_new, -jnp.inf)

        def pv_new(p):
            return jnp.concatenate([_dot(p[kv_rows(g), :], vn[:, g * DF_V:(g + 1) * DF_V])
                                    for g in range(DF_KV_HEADS)], axis=0)

        _merge_partials([_softmax_partial(s_new, pv_new)], m_ref, l_ref, acc_ref)
        o = acc_ref[...] / l_ref[...]
        lam = _lam(lq1_ref[...], lk1_ref[...], lq2_ref[...], lk2_ref[...], lam_init)
        for g in range(DF_KV_HEADS):
            d = o[gc_rows(2 * g), :] - lam * o[gc_rows(2 * g + 1), :]
            y = _rms(d, g_ref[...]) * (1.0 - lam_init)
            for h2 in range(2):
                o_ref[2 * g + h2] = y[_head_rows(h2), :]


def _df_dec(page_table, qdf_s, k_new, v_new, lams, g_df, cache_k_t, cache_v_t, layer, npg, gp, n_new, lam_init):
    db, n_pages = page_table.shape
    n_chunks = n_pages // npg
    rows = 4 * DF_KV_HEADS * DEC_ROWS
    seq = lambda w: pl.BlockSpec((None, DEC_ROWS, w), lambda bi, c, pt: (bi, 0, 0))
    const = lambda shape: pl.BlockSpec(shape, lambda bi, c, pt: (0,) * len(shape))
    in_specs = ([seq(256), seq(LANES), seq(LANES)] + [const((1, DF_D))] * 4 + [const((1, DF_V))]
                + _page_specs((None, None, DF_KV_HEADS, 2, DF_D, LANES), layer, 0, npg, False, n_chunks)
                + _page_specs((None, None, DF_KV_HEADS, DF_V, LANES), layer, 0, npg, False, n_chunks))
    gs = pltpu.PrefetchScalarGridSpec(
        num_scalar_prefetch=1, grid=(db, n_chunks), in_specs=in_specs,
        out_specs=pl.BlockSpec((None, DF_HEADS, DEC_ROWS, DF_V), lambda bi, c, pt: (bi, 0, 0, 0)),
        scratch_shapes=[pltpu.VMEM((2 * DF_KV_HEADS, DF_D, npg * LANES), BF16),
                        pltpu.VMEM((DF_KV_HEADS, DF_V, npg * LANES), BF16),
                        pltpu.VMEM((rows, DF_D), BF16), pltpu.VMEM((rows, 1), F32), pltpu.VMEM((rows, 1), F32),
                        pltpu.VMEM((rows, DF_V), F32)])
    return pl.pallas_call(
        functools.partial(_df_dec_body, npg=npg, gp=gp, n_chunks=n_chunks, n_new=n_new, lam_init=lam_init),
        grid_spec=gs,
        out_shape=jax.ShapeDtypeStruct((db, DF_HEADS, DEC_ROWS, DF_V), F32),
        compiler_params=_cparams(("parallel", "arbitrary")),
        name="df_dec",
    )(page_table, qdf_s, k_new, v_new, *lams, g_df, *([cache_k_t] * npg), *([cache_v_t] * npg))


def _sample_rows(a, db, dt):
    return jnp.pad(a.reshape(db, dt, a.shape[-1]), ((0, 0), (0, DEC_ROWS - dt), (0, 0)))


def _heads_to_rows(o, dt, order=None):
    if order is not None:
        o = o[:, jnp.array(order)]
    db, h, _, w = o.shape
    return jnp.transpose(o[:, :, :dt], (0, 2, 1, 3)).reshape(db * dt, h * w)


_DF_HEAD_ORDER = (0, 2, 1, 3)


def kernel(x_prompt, x_sample, cache_mla_ckv, cache_mla_krope, cache_sb_k, cache_sb_v, cache_df_k, cache_df_v,
           page_table, g_ffn1, w1_gate, w1_up, w1_down, g_mix, w_in, g_q_lat, w_uq, g_kv_lat, w_uk, w_uv,
           g_mla_out, g_sb_out, lam_q1, lam_k1, lam_q2, lam_k2, g_df_sub, w_out, g_ffn2, w2_gate, w2_up, w2_down,
           g_final):
    b, t, d = x_prompt.shape
    db, dt, _ = x_sample.shape
    depth = g_ffn1.shape[0]
    n_pages = page_table.shape[1]
    page = cache_mla_ckv.shape[2]
    assert page == LANES and dt <= DEC_ROWS and d == 1024
    bt, ns = b * t, db * dt
    tq = _pick_tile(t, (512, 256, 128))
    tq_sb = _pick_tile(t, (256, 128))
    tk = _pick_tile(t, (1024, 512, 256, 128))
    npg = _pick_tile(n_pages, (64, 32, 16, 8, 4, 2, 1))
    gp = npg
    sb_recent = 4 if n_pages >= 8 else max(1, n_pages // 2)
    sb_old = n_pages - sb_recent
    sb_npg_old = max((c for c in range(1, 21) if sb_old % c == 0), default=1)

    inv_freq = ROPE_THETA ** (-jnp.arange(0, DF_D, 2, dtype=F32) / DF_D)

    def rope_tables(pos, reps):
        ang = jnp.tile(pos, reps).astype(F32)[:, None] * inv_freq
        return jnp.tile(jnp.cos(ang), (1, LANES // 16)), jnp.tile(jnp.sin(ang), (1, LANES // 16))

    rope_p = rope_tables(jnp.arange(t, dtype=jnp.int32), b)
    rope_s = rope_tables(n_pages * page + jnp.arange(dt, dtype=jnp.int32), db)

    cache_kr_t = jnp.transpose(cache_mla_krope, (0, 1, 3, 2))
    cache_sbk_t = jnp.transpose(cache_sb_k, (0, 1, 3, 4, 2))
    cache_sbv_t = jnp.transpose(cache_sb_v, (0, 1, 3, 4, 2))
    cache_dfk_t = jnp.transpose(cache_df_k, (0, 1, 3, 4, 5, 2))
    cache_dfv_t = jnp.transpose(cache_df_v, (0, 1, 3, 4, 2))

    tri = lambda n: jnp.tril(jnp.ones((n, n), F32)).astype(BF16)
    u_q, u_p, u_new = tri(tq_sb), tri(LANES), tri(DEC_ROWS)

    xp = x_prompt.reshape(bt, d)
    xs = x_sample.reshape(ns, d)
    rows_p = [[] for _ in range(6)]
    rows_s = [[] for _ in range(6)]
    row2 = lambda v: v.reshape(1, -1)
    df_perm = jnp.array([0, 2, 1, 3])

    for l in range(depth):
        lam_init = 0.8 - 0.6 * math.exp(-0.3 * l)
        wi = w_in[l]
        q_df_cols = wi[:, 1696:1952].reshape(d, DF_HEADS, 2 * DF_D)[:, df_perm].reshape(d, 256)
        w_in_r = jnp.concatenate([wi[:, :512], jnp.pad(wi[:, 512:544], ((0, 0), (0, LANES - MLA_ROPE))),
                                  wi[:, 544:1696], q_df_cols, wi[:, 1952:2208]], axis=1).astype(BF16)
        wq = w_uq[l].reshape(256, MLA_HEADS, MLA_NOPE + MLA_ROPE)
        wq_blocks = []
        for hd in range(MLA_HEADS):
            blk = jnp.zeros((256, 256), F32)
            blk = blk.at[:, (hd % 2) * 64:(hd % 2) * 64 + MLA_NOPE].set(wq[:, hd, :MLA_NOPE])
            blk = blk.at[:, LANES:LANES + MLA_ROPE].set(wq[:, hd, MLA_NOPE:])
            wq_blocks.append(blk)
        w_uq_r = jnp.concatenate(wq_blocks, axis=1).astype(BF16)
        w_uk2 = w_uk[l].reshape(256, 384).astype(BF16)
        w_uv2 = w_uv[l].reshape(256, 384).astype(BF16)
        w_ukt = jnp.transpose(w_uk2.reshape(256, 3, LANES), (1, 2, 0))
        wo = w_out[l]
        w_out_r = jnp.concatenate([wo[:768], wo[768:].reshape(DF_HEADS, DF_V, d)[df_perm].reshape(256, d)],
                                  axis=0).astype(BF16)
        lams = (row2(lam_q1[l]), row2(lam_k1[l]), row2(lam_q2[l]), row2(lam_k2[l]))
        g_df = row2(g_df_sub[l])
        g_df2 = jnp.tile(g_df, (1, 2))

        ffn1 = functools.partial(_ffn, g=row2(g_ffn1[l]), wg=w1_gate[l].astype(BF16), wu=w1_up[l].astype(BF16),
                                 wd=w1_down[l].astype(BF16), g_final=row2(g_final), final_norm=False)
        ffn2 = functools.partial(_ffn, g=row2(g_ffn2[l]), wg=w2_gate[l].astype(BF16), wu=w2_up[l].astype(BF16),
                                 wd=w2_down[l].astype(BF16), g_final=row2(g_final), final_norm=l == depth - 1)
        inproj = functools.partial(_inproj, g_mix=row2(g_mix[l]), w_in=w_in_r, g_q=row2(g_q_lat[l]), w_uq=w_uq_r,
                                   g_kv=row2(g_kv_lat[l]), w_uk=w_uk2, w_uv=w_uv2)
        outproj = functools.partial(_outproj, g_mla=row2(g_mla_out[l]), g_sb=row2(g_sb_out[l]), w_out=w_out_r)

        xp = ffn1(xp)
        xs = ffn1(xs)
        new_p = inproj(xp, cos=rope_p[0], sin=rope_p[1])
        new_s = inproj(xs, cos=rope_s[0], sin=rope_s[1])

        (_, _, _, _, _, _, qm, km, vm, qsb, ksb_b, vsb_b, qdf, kdf_b, vdf_b) = new_p
        o_mla_p = _mla_prompt(qm, km, vm, b, t, tq, tk)
        o_sb_p = _sb_prompt(qsb, ksb_b, vsb_b, u_q, b, t, tq_sb)
        o_df_p = _df_prompt(qdf, kdf_b, vdf_b, lams, g_df2, b, t, tq, tk, lam_init)

        ckv, krope, ksb, vsb, kdf, vdf, qm, _, _, qsb, _, _, qdf, _, _ = [_sample_rows(a, db, dt) for a in new_s]
        o_mla_s = _mla_dec(page_table, qm, ckv, krope, w_ukt, w_uv2, cache_mla_ckv, cache_kr_t, l, npg, gp, dt)
        o_sb_s = _sb_dec(page_table, qsb, ksb, vsb, u_p, u_new, cache_sbk_t, cache_sbv_t, l, sb_recent,
                         sb_npg_old, dt).reshape(db, SB_HEADS, DEC_ROWS, SB_DIM)
        o_df_s = _df_dec(page_table, qdf, kdf, vdf, lams, g_df, cache_dfk_t, cache_dfv_t, l, npg, gp, dt, lam_init)

        xp = ffn2(outproj(xp, o_mla_p, o_sb_p, o_df_p))
        xs = ffn2(outproj(xs, _heads_to_rows(o_mla_s, dt), _heads_to_rows(o_sb_s, dt),
                          _heads_to_rows(o_df_s, dt, _DF_HEAD_ORDER)))

        for i in range(6):
            rows_p[i].append(new_p[i])
            rows_s[i].append(new_s[i])

    def stack(rows, lead, tail):
        return jnp.stack(rows, axis=0).reshape((depth,) + lead + tail)

    tails = ((256,), (MLA_ROPE,), (SB_HEADS, SB_DIM), (SB_HEADS, SB_DIM), (DF_KV_HEADS, 2, DF_D), (DF_KV_HEADS, DF_V))
    outs = [xp.reshape(b, t, d), xs.reshape(db, dt, d)]
    for i in range(6):
        outs.append(stack(rows_p[i], (b, t), tails[i]))
        outs.append(stack(rows_s[i], (db, dt), tails[i]))
    return tuple(outs)
```

```python
import functools
import math

import jax
import jax.numpy as jnp
from jax import lax
from jax.experimental import pallas as pl
from jax.experimental.pallas import tpu as pltpu

F32 = jnp.float32
BF16 = jnp.bfloat16

EPS = 1e-6
ROPE_THETA = 10000.0
MLA_HEADS, MLA_NOPE, MLA_ROPE, MLA_V = 6, 64, 32, 64
SB_HEADS, SB_DIM = 6, 64
DF_HEADS, DF_KV_HEADS, DF_D, DF_V = 4, 2, 32, 64
LOG2E = math.log2(math.e)
MLA_SCALE = (MLA_NOPE + MLA_ROPE) ** -0.5 * LOG2E
DF_SCALE = DF_D ** -0.5 * LOG2E
SB_SCALE = SB_DIM ** -0.5
SB_DEAD = -104.0

LANES = 128
DEC_ROWS = 16
VMEM_LIMIT = 56 * 1024 * 1024
NEG_BIG = -1e30


def _cparams(sem):
    return pltpu.CompilerParams(dimension_semantics=sem, vmem_limit_bytes=VMEM_LIMIT)


def _rms(x, g):
    return x * lax.rsqrt(jnp.mean(x * x, axis=-1, keepdims=True) + EPS) * g


def _dot(a, b):
    return jnp.dot(a, b, preferred_element_type=F32)


def _dot_t(a, b):
    return lax.dot_general(a, b, (((1,), (1,)), ((), ())), preferred_element_type=F32)


def _pick_tile(n, candidates):
    for c in candidates:
        if n % c == 0:
            return c
    raise ValueError(f"no tile for {n}")


def _full(shape):
    nd = len(shape)
    return pl.BlockSpec(shape, lambda *_: (0,) * nd)


def _swiglu_block(x, g_ref, wg_ref, wu_ref, wd_ref, gf_ref, o_ref, ff_chunk, final_norm):
    h = _rms(x, g_ref[...]).astype(BF16)
    acc = jnp.zeros_like(x)
    d_ff = wg_ref.shape[1]
    for c0 in range(0, d_ff, ff_chunk):
        gate = _dot(h, wg_ref[:, c0:c0 + ff_chunk])
        up = _dot(h, wu_ref[:, c0:c0 + ff_chunk])
        act = (gate * jax.nn.sigmoid(gate) * up).astype(BF16)
        acc = acc + _dot(act, wd_ref[c0:c0 + ff_chunk, :])
    y = x + 0.5 * acc
    if final_norm:
        y = _rms(y, gf_ref[...])
    o_ref[...] = y


def _ffn_body(x_ref, g_ref, wg_ref, wu_ref, wd_ref, gf_ref, o_ref, *, ff_chunk, final_norm):
    _swiglu_block(x_ref[...], g_ref, wg_ref, wu_ref, wd_ref, gf_ref, o_ref, ff_chunk, final_norm)


def _mix_ffn_body(x_ref, om_ref, os_ref, od_ref, gm_ref, gs_ref, w_ref, g_ref, wg_ref, wu_ref, wd_ref, gf_ref, o_ref,
                  *, ff_chunk, final_norm):
    mixed_m = _rms(om_ref[...], gm_ref[...]).astype(BF16)
    mixed_s = _rms(os_ref[...], gs_ref[...]).astype(BF16)
    mixed_d = od_ref[...].astype(BF16)
    y = _dot(mixed_m, w_ref[0:384, :]) + _dot(mixed_s, w_ref[384:768, :]) + _dot(mixed_d, w_ref[768:1024, :])
    _swiglu_block(x_ref[...] + y, g_ref, wg_ref, wu_ref, wd_ref, gf_ref, o_ref, ff_chunk, final_norm)


def _mix_ffn(x, o_mla, o_sb, o_df, g_mla, g_sb, w_out, g, wg, wu, wd, g_final, final_norm):
    n, d = x.shape
    d_ff = wg.shape[1]
    tm = _pick_tile(n, (512, 256, 128, 64, 32, 16, 8))
    ff_chunk = d_ff // 2 if (d_ff // 2) % LANES == 0 else d_ff
    row = lambda w: pl.BlockSpec((tm, w), lambda i: (i, 0))
    return pl.pallas_call(
        functools.partial(_mix_ffn_body, ff_chunk=ff_chunk, final_norm=final_norm),
        grid=(n // tm,),
        in_specs=[row(d), row(384), row(384), row(256), _full((1, 384)), _full((1, 384)), _full(w_out.shape),
                  _full((1, d)), _full((d, d_ff)), _full((d, d_ff)), _full((d_ff, d)), _full((1, d))],
        out_specs=row(d),
        out_shape=jax.ShapeDtypeStruct((n, d), F32),
        compiler_params=_cparams(("parallel",)),
        name="mix_ffn",
    )(x, o_mla, o_sb, o_df, g_mla, g_sb, w_out, g, wg, wu, wd, g_final)


def _ffn(x, g, wg, wu, wd, g_final, final_norm):
    n, d = x.shape
    d_ff = wg.shape[1]
    tm = _pick_tile(n, (512, 256, 128, 64, 32, 16, 8))
    ff_chunk = d_ff // 2 if (d_ff // 2) % LANES == 0 else d_ff
    row = pl.BlockSpec((tm, d), lambda i: (i, 0))
    return pl.pallas_call(
        functools.partial(_ffn_body, ff_chunk=ff_chunk, final_norm=final_norm),
        grid=(n // tm,),
        in_specs=[row, _full((1, d)), _full((d, d_ff)), _full((d, d_ff)), _full((d_ff, d)), _full((1, d))],
        out_specs=row,
        out_shape=jax.ShapeDtypeStruct((n, d), F32),
        compiler_params=_cparams(("parallel",)),
        name="ffn",
    )(x, g, wg, wu, wd, g_final)


def _rope128(x, cos, sin):
    lane = lax.broadcasted_iota(jnp.int32, x.shape, 1)
    first = (lane % 32) < 16
    rot = jnp.where(first, -pltpu.roll(x, LANES - 16, 1), pltpu.roll(x, 16, 1))
    return x * cos + rot * sin


_CQ, _CKV, _KR, _QSB, _KSB, _VSB, _QDF, _KDF, _VDF, _PROJ_W = 0, 256, 512, 640, 1024, 1408, 1792, 2048, 2176, 2304


def _inproj_body(x_ref, g_ref, win_ref, gq_ref, wuq_ref, gkv_ref, wuk_ref, wuv_ref, cos_ref, sin_ref,
                 ckv_o, krope_o, ksb_o, vsb_o, kdf_o, vdf_o,
                 qm_o, km_o, vm_o, qsb_o, ksbb_o, vsbb_o, qdf_o, kdfb_o, vdfb_o):
    h = _rms(x_ref[...], g_ref[...]).astype(BF16)
    proj = _dot(h, win_ref[...])
    cos = cos_ref[...]
    sin = sin_ref[...]

    cq = _rms(proj[:, _CQ:_CQ + 256], gq_ref[...]).astype(BF16)
    q = _dot(cq, wuq_ref[...])
    for hd in range(MLA_HEADS):
        base = hd * 256
        qm_o[:, base:base + LANES] = (q[:, base:base + LANES] * MLA_SCALE).astype(BF16)
        qr = _rope128(q[:, base + LANES:base + 2 * LANES], cos, sin)
        qm_o[:, base + LANES:base + 2 * LANES] = (qr * MLA_SCALE).astype(BF16)
    ckv = _rms(proj[:, _CKV:_CKV + 256], gkv_ref[...])
    ckv_o[...] = ckv
    ckv_b = ckv.astype(BF16)
    k_nope = _dot(ckv_b, wuk_ref[...])
    vm_o[...] = _dot(ckv_b, wuv_ref[...]).astype(BF16)
    krope = _rope128(proj[:, _KR:_KR + LANES], cos, sin)
    krope_o[...] = krope[:, :MLA_ROPE]
    krope_b = krope.astype(BF16)
    for p in range(MLA_HEADS // 2):
        km_o[:, p * 256:p * 256 + LANES] = k_nope[:, p * LANES:(p + 1) * LANES].astype(BF16)
        km_o[:, p * 256 + LANES:(p + 1) * 256] = krope_b

    qsb_o[...] = (proj[:, _QSB:_QSB + 384] * SB_SCALE).astype(BF16)
    ksb = proj[:, _KSB:_KSB + 384]
    vsb = proj[:, _VSB:_VSB + 384]
    ksb_o[...] = ksb
    vsb_o[...] = vsb
    ksbb_o[...] = ksb.astype(BF16)
    vsbb_o[...] = vsb.astype(BF16)

    for c in range(2):
        qd = _rope128(proj[:, _QDF + c * LANES:_QDF + (c + 1) * LANES], cos, sin)
        qdf_o[:, c * LANES:(c + 1) * LANES] = (qd * DF_SCALE).astype(BF16)
    kdf = _rope128(proj[:, _KDF:_KDF + LANES], cos, sin)
    vdf = proj[:, _VDF:_VDF + LANES]
    kdf_o[...] = kdf
    vdf_o[...] = vdf
    kdfb_o[...] = kdf.astype(BF16)
    vdfb_o[...] = vdf.astype(BF16)


def _inproj(x, g_mix, w_in, g_q, w_uq, g_kv, w_uk, w_uv, cos, sin):
    n, d = x.shape
    tm = _pick_tile(n, (512, 256, 128, 64, 32, 16, 8))
    row = lambda w: pl.BlockSpec((tm, w), lambda i: (i, 0))
    widths_f32 = (256, MLA_ROPE, 384, 384, LANES, LANES)
    widths_bf16 = (MLA_HEADS * 256, 3 * 256, 384, 384, 384, 384, 256, LANES, LANES)
    out_shape = ([jax.ShapeDtypeStruct((n, w), F32) for w in widths_f32]
                 + [jax.ShapeDtypeStruct((n, w), BF16) for w in widths_bf16])
    return pl.pallas_call(
        _inproj_body,
        grid=(n // tm,),
        in_specs=[row(d), _full((1, d)), _full(w_in.shape), _full((1, 256)), _full(w_uq.shape), _full((1, 256)),
                  _full(w_uk.shape), _full(w_uv.shape), row(LANES), row(LANES)],
        out_specs=[row(w) for w in widths_f32 + widths_bf16],
        out_shape=out_shape,
        compiler_params=_cparams(("parallel",)),
        name="inproj",
    )(x, g_mix, w_in, g_q, w_uq, g_kv, w_uk, w_uv, cos, sin)


def _softmax_attend(qs, k_ref, v_ref, qi, tq, tk):
    def step(off, carry, masked, width=tk):
        k = k_ref[pl.ds(off, width), :]
        v = v_ref[pl.ds(off, width), :]
        if masked:
            row = lax.broadcasted_iota(jnp.int32, (tq, width), 0) + qi * tq
            col = lax.broadcasted_iota(jnp.int32, (tq, width), 1) + off
            valid = col <= row
        out = []
        for q, (m, l, acc) in zip(qs, carry):
            s = _dot_t(q, k)
            if masked:
                s = jnp.where(valid, s, -jnp.inf)
            m_new = jnp.maximum(m, jnp.max(s, axis=-1, keepdims=True))
            p = jnp.exp2(s - m_new)
            alpha = jnp.exp2(m - m_new)
            l = alpha * l + jnp.sum(p, axis=-1, keepdims=True)
            acc = alpha * acc + _dot(p.astype(BF16), v)
            out.append((m_new, l, acc))
        return tuple(out)

    init = tuple((jnp.full((tq, 1), NEG_BIG, F32), jnp.zeros((tq, 1), F32), jnp.zeros((tq, LANES), F32)) for _ in qs)
    n_full = (qi * tq) // tk
    carry = lax.fori_loop(0, n_full, lambda j, c: step(pl.multiple_of(j * tk, tk), c, False), init)
    off = pl.multiple_of(n_full * tk, tk)
    if tk == 2 * tq:
        carry = lax.cond(qi * tq == n_full * tk,
                         lambda c: step(off, c, True, tq), lambda c: step(off, c, True), carry)
    else:
        carry = step(off, carry, True)
    return [(acc, l) for _, l, acc in carry]


def _mla_prompt_body(q_ref, k_ref, v_ref, o_ref, *, tq, tk):
    qi = pl.program_id(2)
    lane = lax.broadcasted_iota(jnp.int32, (tq, LANES), 1)
    (acc0, l0), (acc1, l1) = _softmax_attend([q_ref[:, 0:256], q_ref[:, 256:512]], k_ref, v_ref, qi, tq, tk)
    o_ref[...] = jnp.where(lane < 64, acc0 / l0, acc1 / l1)


def _mla_prompt(qm, km, vm, b, t, tq, tk):
    nq = t // tq
    return pl.pallas_call(
        functools.partial(_mla_prompt_body, tq=tq, tk=tk),
        grid=(b, MLA_HEADS // 2, nq),
        in_specs=[pl.BlockSpec((tq, 512), lambda bi, p, qi: (bi * nq + qi, p)),
                  pl.BlockSpec((t, 256), lambda bi, p, qi: (bi, p)),
                  pl.BlockSpec((t, LANES), lambda bi, p, qi: (bi, p))],
        out_specs=pl.BlockSpec((tq, LANES), lambda bi, p, qi: (bi * nq + qi, p)),
        out_shape=jax.ShapeDtypeStruct((b * t, 384), F32),
        compiler_params=_cparams(("parallel", "parallel", "arbitrary")),
        name="mla_prompt",
    )(qm, km, vm)


def _log_keep(z):
    return jnp.minimum(-z, 0.0) - jnp.log1p(jnp.exp(-jnp.abs(z)))


def _suffix_sum(lk, u):
    hi = lk.astype(BF16)
    lo = (lk - hi.astype(F32)).astype(BF16)
    return _dot(hi, u) + _dot(lo, u)


def _sb_weights(z, mask, c, u):
    lk = _log_keep(z)
    if mask is not None:
        lk = jnp.where(mask, lk, 0.0)
    incl = _suffix_sum(lk, u)
    a = jnp.exp(z + incl + c)
    if mask is not None:
        a = jnp.where(mask, a, 0.0)
    return a, c + incl[:, 0:1]


def _any_alive(cs):
    return functools.reduce(jnp.maximum, [jnp.max(c) for c in cs]) > SB_DEAD


def _sb_attend(qs, k_ref, v_ref, u, qi, tq):
    def step(off, carry, mask):
        out = []
        for (q, col0), (c, acc) in zip(qs, carry):
            k = k_ref[pl.ds(off, tq), col0:col0 + LANES]
            v = v_ref[pl.ds(off, tq), col0:col0 + LANES]
            a, c = _sb_weights(_dot_t(q, k), mask, c, u)
            out.append((c, acc + _dot(a.astype(BF16), v)))
        return tuple(out)

    row = lax.broadcasted_iota(jnp.int32, (tq, tq), 0)
    col = lax.broadcasted_iota(jnp.int32, (tq, tq), 1)
    init = tuple((jnp.zeros((tq, 1), F32), jnp.zeros((tq, LANES), F32)) for _ in qs)
    carry = step(pl.multiple_of(qi * tq, tq), init, col < row)

    def cond(state):
        i, carry = state
        return jnp.logical_and(i < qi, _any_alive([c for c, _ in carry]))

    def body(state):
        i, carry = state
        return i + 1, step(pl.multiple_of((qi - 1 - i) * tq, tq), carry, None)

    _, carry = lax.while_loop(cond, body, (jnp.int32(0), carry))
    return [acc for _, acc in carry]


def _sb_prompt_body(q_ref, k_ref, v_ref, u_ref, o_ref, *, tq):
    qi = pl.program_id(1)
    lane = lax.broadcasted_iota(jnp.int32, (tq, LANES), 1)
    qs = []
    for p in range(SB_HEADS // 2):
        q = q_ref[:, p * LANES:(p + 1) * LANES].astype(F32)
        qs += [(jnp.where((lane // 64) == hh, q, 0.0).astype(BF16), p * LANES) for hh in range(2)]
    acc = _sb_attend(qs, k_ref, v_ref, u_ref[...], qi, tq)
    for p in range(SB_HEADS // 2):
        o_ref[:, p * LANES:(p + 1) * LANES] = jnp.where(lane < 64, acc[2 * p], acc[2 * p + 1])


def _sb_prompt(qsb, ksb, vsb, u, b, t, tq):
    nq = t // tq
    width = SB_HEADS * SB_DIM
    return pl.pallas_call(
        functools.partial(_sb_prompt_body, tq=tq),
        grid=(b, nq),
        in_specs=[pl.BlockSpec((tq, width), lambda bi, qi: (bi * nq + qi, 0)),
                  pl.BlockSpec((t, width), lambda bi, qi: (bi, 0)),
                  pl.BlockSpec((t, width), lambda bi, qi: (bi, 0)),
                  _full((tq, tq))],
        out_specs=pl.BlockSpec((tq, width), lambda bi, qi: (bi * nq + qi, 0)),
        out_shape=jax.ShapeDtypeStruct((b * t, width), F32),
        compiler_params=_cparams(("parallel", "arbitrary")),
        name="sb_prompt",
    )(qsb, ksb, vsb, u)


def _lam(lq1, lk1, lq2, lk2, lam_init):
    return (jnp.exp(jnp.sum(lq1 * lk1, axis=-1, keepdims=True))
            - jnp.exp(jnp.sum(lq2 * lk2, axis=-1, keepdims=True)) + lam_init)


def _df_prompt_body(q_ref, k_ref, v_ref, lq1_ref, lk1_ref, lq2_ref, lk2_ref, g_ref, o_ref, *, tq, tk, lam_init):
    qi = pl.program_id(2)
    lane = lax.broadcasted_iota(jnp.int32, (tq, LANES), 1)
    q = q_ref[...].astype(F32)
    lam = _lam(lq1_ref[...], lk1_ref[...], lq2_ref[...], lk2_ref[...], lam_init)
    qs = [jnp.where((lane // DF_D) == gc, q, 0.0).astype(BF16) for gc in range(2 * DF_KV_HEADS)]
    res = _softmax_attend(qs, k_ref, v_ref, qi, tq, tk)
    out = jnp.zeros((tq, LANES), F32)
    for g in range(DF_KV_HEADS):
        (acc1, l1), (acc2, l2) = res[2 * g], res[2 * g + 1]
        own = (lane // 64) == g
        d = jnp.where(own, acc1 / l1 - lam * (acc2 / l2), 0.0)
        ms = jnp.sum(d * d, axis=-1, keepdims=True) * (1.0 / DF_V)
        y = d * lax.rsqrt(ms + EPS) * g_ref[...] * (1.0 - lam_init)
        out = jnp.where(own, y, out)
    o_ref[...] = out


def _df_prompt(qdf, kdf, vdf, lams, g_df2, b, t, tq, tk, lam_init):
    nq = t // tq
    return pl.pallas_call(
        functools.partial(_df_prompt_body, tq=tq, tk=tk, lam_init=lam_init),
        grid=(b, 2, nq),
        in_specs=[pl.BlockSpec((tq, LANES), lambda bi, r, qi: (bi * nq + qi, r)),
                  pl.BlockSpec((t, LANES), lambda bi, r, qi: (bi, 0)),
                  pl.BlockSpec((t, LANES), lambda bi, r, qi: (bi, 0)),
                  _full((1, DF_D)), _full((1, DF_D)), _full((1, DF_D)), _full((1, DF_D)), _full((1, LANES))],
        out_specs=pl.BlockSpec((tq, LANES), lambda bi, r, qi: (bi * nq + qi, r)),
        out_shape=jax.ShapeDtypeStruct((b * t, 256), F32),
        compiler_params=_cparams(("parallel", "parallel", "arbitrary")),
        name="df_prompt",
    )(qdf, kdf, vdf, *lams, g_df2)


def _page_specs(block, layer, first_page, npg, reverse_chunks, n_chunks):
    nz = len(block) - 2

    def imap(bi, c, pt, *, slot):
        chunk = (n_chunks - 1 - c) if reverse_chunks else c
        return (layer, pt[bi, first_page + chunk * npg + slot]) + (0,) * nz

    return [pl.BlockSpec(block, functools.partial(imap, slot=s)) for s in range(npg)]


def _softmax_partial(s, pv):
    m = jnp.max(s, axis=-1, keepdims=True)
    p = jnp.exp2(s - m)
    return m, jnp.sum(p, axis=-1, keepdims=True), pv(p.astype(BF16))


def _merge_partials(parts, m_ref, l_ref, acc_ref):
    m_old = m_ref[...]
    m_new = functools.reduce(jnp.maximum, [m for m, _, _ in parts], m_old)
    w_old = jnp.exp2(m_old - m_new)
    l = w_old * l_ref[...]
    acc = w_old * acc_ref[...]
    for m, lp, ap in parts:
        w = jnp.exp2(m - m_new)
        l = l + w * lp
        acc = acc + w * ap
    m_ref[...] = m_new
    l_ref[...] = l
    acc_ref[...] = acc


def _new_key_mask(rows, n_new, strict):
    t = lax.broadcasted_iota(jnp.int32, (rows, DEC_ROWS), 0) % DEC_ROWS
    kk = lax.broadcasted_iota(jnp.int32, (rows, DEC_ROWS), 1)
    return ((kk < t) if strict else (kk <= t)) & (kk < n_new)


def _head_rows(hd):
    return slice(hd * DEC_ROWS, (hd + 1) * DEC_ROWS)


def _mla_dec_body(pt_ref, q_ref, cnew_ref, rnew_ref, wukt_ref, wuv_ref, *rest, npg, gp, n_chunks, n_new):
    ckv_pages = rest[:npg]
    kr_pages = rest[npg:2 * npg]
    o_ref = rest[2 * npg]
    kvb, krb, qlat, qrope, m_ref, l_ref, acc_ref = rest[2 * npg + 1:]
    c = pl.program_id(1)
    rows = MLA_HEADS * DEC_ROWS

    @pl.when(c == 0)
    def _():
        for hd in range(MLA_HEADS):
            base = hd * 256
            ql = _dot(q_ref[:, base:base + LANES], wukt_ref[hd // 2])
            qlat[_head_rows(hd), :] = ql.astype(BF16)
            qrope[_head_rows(hd), :] = q_ref[:, base + LANES:base + 2 * LANES]
        m_ref[...] = jnp.full(m_ref.shape, NEG_BIG, F32)
        l_ref[...] = jnp.zeros(l_ref.shape, F32)
        acc_ref[...] = jnp.zeros(acc_ref.shape, F32)

    ql = qlat[...]
    qr = qrope[:, :MLA_ROPE]
    parts = []
    for g0 in range(0, npg, gp):
        for i in range(g0, g0 + gp):
            kvb[i * LANES:(i + 1) * LANES, :] = ckv_pages[i][...].astype(BF16)
            krb[:, i * LANES:(i + 1) * LANES] = kr_pages[i][...].astype(BF16)
        span = slice(g0 * LANES, (g0 + gp) * LANES)
        s = _dot_t(ql, kvb[span, :]) + _dot(qr, krb[:, span])
        parts.append(_softmax_partial(s, lambda p, span=span: _dot(p, kvb[span, :])))
    _merge_partials(parts, m_ref, l_ref, acc_ref)

    @pl.when(c == n_chunks - 1)
    def _():
        cn = cnew_ref[...].astype(BF16)
        rn = rnew_ref[...].astype(BF16)
        s_new = _dot_t(ql, cn) + _dot_t(qr, rn)
        s_new = jnp.where(_new_key_mask(rows, n_new, False), s_new, -jnp.inf)
        _merge_partials([_softmax_partial(s_new, lambda p: _dot(p, cn))], m_ref, l_ref, acc_ref)
        o_lat = (acc_ref[...] / l_ref[...]).astype(BF16)
        for hd in range(MLA_HEADS):
            o_ref[hd] = _dot(o_lat[_head_rows(hd), :], wuv_ref[:, hd * MLA_V:(hd + 1) * MLA_V])


def _mla_dec(page_table, qm_s, ckv_new, kr_new, w_ukt, w_uv, cache_ckv, cache_kr_t, layer, npg, gp, n_new):
    db, n_pages = page_table.shape
    n_chunks = n_pages // npg
    rows = MLA_HEADS * DEC_ROWS
    seq = lambda w: pl.BlockSpec((None, DEC_ROWS, w), lambda bi, c, pt: (bi, 0, 0))
    const = lambda shape: pl.BlockSpec(shape, lambda bi, c, pt: (0,) * len(shape))
    in_specs = ([seq(MLA_HEADS * 256), seq(256), seq(MLA_ROPE), const(w_ukt.shape), const(w_uv.shape)]
                + _page_specs((None, None, LANES, 256), layer, 0, npg, False, n_chunks)
                + _page_specs((None, None, MLA_ROPE, LANES), layer, 0, npg, False, n_chunks))
    gs = pltpu.PrefetchScalarGridSpec(
        num_scalar_prefetch=1, grid=(db, n_chunks), in_specs=in_specs,
        out_specs=pl.BlockSpec((None, MLA_HEADS, DEC_ROWS, MLA_V), lambda bi, c, pt: (bi, 0, 0, 0)),
        scratch_shapes=[pltpu.VMEM((npg * LANES, 256), BF16), pltpu.VMEM((MLA_ROPE, npg * LANES), BF16),
                        pltpu.VMEM((rows, 256), BF16), pltpu.VMEM((rows, LANES), BF16),
                        pltpu.VMEM((rows, 1), F32), pltpu.VMEM((rows, 1), F32), pltpu.VMEM((rows, 256), F32)])
    return pl.pallas_call(
        functools.partial(_mla_dec_body, npg=npg, gp=gp, n_chunks=n_chunks, n_new=n_new),
        grid_spec=gs,
        out_shape=jax.ShapeDtypeStruct((db, MLA_HEADS, DEC_ROWS, MLA_V), F32),
        compiler_params=_cparams(("parallel", "arbitrary")),
        name="mla_dec",
    )(page_table, qm_s, ckv_new, kr_new, w_ukt, w_uv, *([cache_ckv] * npg), *([cache_kr_t] * npg))


def _sb_pages(qs, k_pages, v_pages, ktb, vtb, u, c, acc_ref):
    npg = len(k_pages)
    for i in range(npg):
        for hd in range(SB_HEADS):
            ktb[hd, :, i * LANES:(i + 1) * LANES] = k_pages[i][hd].astype(BF16)
            vtb[hd, :, i * LANES:(i + 1) * LANES] = v_pages[i][hd].astype(BF16)
    z = jnp.concatenate([_dot(qs[hd], ktb[hd]) for hd in range(SB_HEADS)], axis=0)
    a = [None] * npg
    for blk in range(npg - 1, -1, -1):
        a[blk], c = _sb_weights(z[:, blk * LANES:(blk + 1) * LANES], None, c, u)
    a = jnp.concatenate(a, axis=1).astype(BF16)
    for hd in range(SB_HEADS):
        acc_ref[_head_rows(hd), :] += _dot_t(a[_head_rows(hd), :], vtb[hd])
    return c


def _sb_dec_new_body(pt_ref, q_ref, knew_ref, vnew_ref, u_ref, un_ref, *rest, npg, n_new):
    k_pages = rest[:npg]
    v_pages = rest[npg:2 * npg]
    acc_o, c_o, ktb, vtb = rest[2 * npg:]
    rows = SB_HEADS * DEC_ROWS
    q = q_ref[...].astype(F32)
    kn = knew_ref[...]
    vn = vnew_ref[...]
    head = lambda x, hd: x[:, hd * SB_DIM:(hd + 1) * SB_DIM].astype(BF16)
    qs = [head(q, hd) for hd in range(SB_HEADS)]
    z = jnp.concatenate([_dot_t(qs[hd], head(kn, hd)) for hd in range(SB_HEADS)], axis=0)
    a, c = _sb_weights(z, _new_key_mask(rows, n_new, True), jnp.zeros((rows, 1), F32), un_ref[...])
    a = a.astype(BF16)
    for hd in range(SB_HEADS):
        acc_o[_head_rows(hd), :] = _dot(a[_head_rows(hd), :], head(vn, hd))
    c_o[...] = _sb_pages(qs, k_pages, v_pages, ktb, vtb, u_ref[...], c, acc_o)


def _sb_dec_old_body(pt_ref, q_ref, accin_ref, cin_ref, u_ref, *rest, npg):
    k_pages = rest[:npg]
    v_pages = rest[npg:2 * npg]
    acc_o, ktb, vtb, c_ref = rest[2 * npg:]
    ci = pl.program_id(1)

    @pl.when(ci == 0)
    def _():
        acc_o[...] = accin_ref[...]
        c_ref[...] = cin_ref[...]

    @pl.when(_any_alive([c_ref[...]]))
    def _():
        q = q_ref[...].astype(F32)
        qs = [q[:, hd * SB_DIM:(hd + 1) * SB_DIM].astype(BF16) for hd in range(SB_HEADS)]
        c_ref[...] = _sb_pages(qs, k_pages, v_pages, ktb, vtb, u_ref[...], c_ref[...], acc_o)


def _sb_dec(page_table, qsb_s, k_new, v_new, u, u_new, cache_k_t, cache_v_t, layer, n_recent, npg_old, n_new):
    db, n_pages = page_table.shape
    rows = SB_HEADS * DEC_ROWS
    page = (None, None, SB_HEADS, SB_DIM, LANES)
    seq = lambda w: pl.BlockSpec((None, DEC_ROWS, w), lambda bi, c, pt: (bi, 0, 0))
    per_seq = lambda w: pl.BlockSpec((None, rows, w), lambda bi, c, pt: (bi, 0, 0))
    const = lambda shape: pl.BlockSpec(shape, lambda bi, c, pt: (0,) * len(shape))
    tb = lambda n: [pltpu.VMEM((SB_HEADS, SB_DIM, n * LANES), BF16)] * 2
    n_old = n_pages - n_recent

    gs = pltpu.PrefetchScalarGridSpec(
        num_scalar_prefetch=1, grid=(db, 1),
        in_specs=([seq(384), seq(384), seq(384), const(u.shape), const(u_new.shape)]
                  + _page_specs(page, layer, n_old, n_recent, False, 1) * 2),
        out_specs=[per_seq(SB_DIM), per_seq(1)],
        scratch_shapes=tb(n_recent))
    acc, c = pl.pallas_call(
        functools.partial(_sb_dec_new_body, npg=n_recent, n_new=n_new),
        grid_spec=gs,
        out_shape=[jax.ShapeDtypeStruct((db, rows, SB_DIM), F32), jax.ShapeDtypeStruct((db, rows, 1), F32)],
        compiler_params=_cparams(("parallel", "arbitrary")),
        name="sb_dec_new",
    )(page_table, qsb_s, k_new, v_new, u, u_new, *([cache_k_t] * n_recent), *([cache_v_t] * n_recent))
    if n_old == 0:
        return acc

    n_chunks = n_old // npg_old
    gs_old = pltpu.PrefetchScalarGridSpec(
        num_scalar_prefetch=1, grid=(db, n_chunks),
        in_specs=([seq(384), per_seq(SB_DIM), per_seq(1), const(u.shape)]
                  + _page_specs(page, layer, 0, npg_old, True, n_chunks) * 2),
        out_specs=per_seq(SB_DIM),
        scratch_shapes=tb(npg_old) + [pltpu.VMEM((rows, 1), F32)])

    def older():
        return pl.pallas_call(
            functools.partial(_sb_dec_old_body, npg=npg_old),
            grid_spec=gs_old,
            out_shape=jax.ShapeDtypeStruct((db, rows, SB_DIM), F32),
            compiler_params=_cparams(("parallel", "arbitrary")),
            name="sb_dec_old",
        )(page_table, qsb_s, acc, c, u, *([cache_k_t] * npg_old), *([cache_v_t] * npg_old))

    return lax.cond(jnp.max(c) > SB_DEAD, older, lambda: acc)


def _df_q_offset(hd, c):
    return (hd % 2) * LANES + (hd // 2) * 64 + c * DF_D


def _df_dec_body(pt_ref, q_ref, knew_ref, vnew_ref, lq1_ref, lk1_ref, lq2_ref, lk2_ref, g_ref, *rest,
                 npg, gp, n_chunks, n_new, lam_init):
    k_pages = rest[:npg]
    v_pages = rest[npg:2 * npg]
    o_ref = rest[2 * npg]
    ktb, vtb, qs, m_ref, l_ref, acc_ref = rest[2 * npg + 1:]
    c = pl.program_id(1)
    grp = 2 * DEC_ROWS
    rows = 2 * DF_KV_HEADS * grp

    @pl.when(c == 0)
    def _():
        q = q_ref[...].astype(F32)
        pieces = []
        for g in range(DF_KV_HEADS):
            for cc in range(2):
                for h2 in range(2):
                    off = _df_q_offset(2 * g + h2, cc)
                    pieces.append(q[:, off:off + DF_D])
        qs[...] = jnp.concatenate(pieces, axis=0).astype(BF16)
        m_ref[...] = jnp.full(m_ref.shape, NEG_BIG, F32)
        l_ref[...] = jnp.zeros(l_ref.shape, F32)
        acc_ref[...] = jnp.zeros(acc_ref.shape, F32)

    def gc_rows(gc):
        return slice(gc * grp, (gc + 1) * grp)

    def kv_rows(g):
        return slice(g * 2 * grp, (g + 1) * 2 * grp)

    parts = []
    for g0 in range(0, npg, gp):
        for i in range(g0, g0 + gp):
            for g in range(DF_KV_HEADS):
                vtb[g, :, i * LANES:(i + 1) * LANES] = v_pages[i][g].astype(BF16)
                for cc in range(2):
                    ktb[2 * g + cc, :, i * LANES:(i + 1) * LANES] = k_pages[i][g, cc].astype(BF16)
        span = slice(g0 * LANES, (g0 + gp) * LANES)
        s = jnp.concatenate([_dot(qs[gc_rows(gc), :], ktb[gc, :, span]) for gc in range(2 * DF_KV_HEADS)], axis=0)

        def pv_pages(p, span=span):
            return jnp.concatenate([_dot_t(p[kv_rows(g), :], vtb[g, :, span]) for g in range(DF_KV_HEADS)], axis=0)

        parts.append(_softmax_partial(s, pv_pages))
    _merge_partials(parts, m_ref, l_ref, acc_ref)

    @pl.when(c == n_chunks - 1)
    def _():
        kn = knew_ref[...].astype(BF16)
        vn = vnew_ref[...].astype(BF16)
        s_new = jnp.concatenate([_dot_t(qs[gc_rows(gc), :], kn[:, gc * DF_D:(gc + 1) * DF_D])
                                 for gc in range(2 * DF_KV_HEADS)], axis=0)
        s_new = jnp.where(_new_key_mask(rows, n_new, False), s_new, -jnp.inf)

        def pv_new(p):
            return jnp.concatenate([_dot(p[kv_rows(g), :], vn[:, g * DF_V:(g + 1) * DF_V])
                                    for g in range(DF_KV_HEADS)], axis=0)

        _merge_partials([_softmax_partial(s_new, pv_new)], m_ref, l_ref, acc_ref)
        o = acc_ref[...] / l_ref[...]
        lam = _lam(lq1_ref[...], lk1_ref[...], lq2_ref[...], lk2_ref[...], lam_init)
        for g in range(DF_KV_HEADS):
            d = o[gc_rows(2 * g), :] - lam * o[gc_rows(2 * g + 1), :]
            y = _rms(d, g_ref[...]) * (1.0 - lam_init)
            for h2 in range(2):
                o_ref[2 * g + h2] = y[_head_rows(h2), :]


def _df_dec(page_table, qdf_s, k_new, v_new, lams, g_df, cache_k_t, cache_v_t, layer, npg, gp, n_new, lam_init):
    db, n_pages = page_table.shape
    n_chunks = n_pages // npg
    rows = 4 * DF_KV_HEADS * DEC_ROWS
    seq = lambda w: pl.BlockSpec((None, DEC_ROWS, w), lambda bi, c, pt: (bi, 0, 0))
    const = lambda shape: pl.BlockSpec(shape, lambda bi, c, pt: (0,) * len(shape))
    in_specs = ([seq(256), seq(LANES), seq(LANES)] + [const((1, DF_D))] * 4 + [const((1, DF_V))]
                + _page_specs((None, None, DF_KV_HEADS, 2, DF_D, LANES), layer, 0, npg, False, n_chunks)
                + _page_specs((None, None, DF_KV_HEADS, DF_V, LANES), layer, 0, npg, False, n_chunks))
    gs = pltpu.PrefetchScalarGridSpec(
        num_scalar_prefetch=1, grid=(db, n_chunks), in_specs=in_specs,
        out_specs=pl.BlockSpec((None, DF_HEADS, DEC_ROWS, DF_V), lambda bi, c, pt: (bi, 0, 0, 0)),
        scratch_shapes=[pltpu.VMEM((2 * DF_KV_HEADS, DF_D, npg * LANES), BF16),
                        pltpu.VMEM((DF_KV_HEADS, DF_V, npg * LANES), BF16),
                        pltpu.VMEM((rows, DF_D), BF16), pltpu.VMEM((rows, 1), F32), pltpu.VMEM((rows, 1), F32),
                        pltpu.VMEM((rows, DF_V), F32)])
    return pl.pallas_call(
        functools.partial(_df_dec_body, npg=npg, gp=gp, n_chunks=n_chunks, n_new=n_new, lam_init=lam_init),
        grid_spec=gs,
        out_shape=jax.ShapeDtypeStruct((db, DF_HEADS, DEC_ROWS, DF_V), F32),
        compiler_params=_cparams(("parallel", "arbitrary")),
        name="df_dec",
    )(page_table, qdf_s, k_new, v_new, *lams, g_df, *([cache_k_t] * npg), *([cache_v_t] * npg))


def _sample_rows(a, db, dt):
    return jnp.pad(a.reshape(db, dt, a.shape[-1]), ((0, 0), (0, DEC_ROWS - dt), (0, 0)))


def _heads_to_rows(o, dt, order=None):
    if order is not None:
        o = o[:, jnp.array(order)]
    db, h, _, w = o.shape
    return jnp.transpose(o[:, :, :dt], (0, 2, 1, 3)).reshape(db * dt, h * w)


_DF_HEAD_ORDER = (0, 2, 1, 3)


def kernel(x_prompt, x_sample, cache_mla_ckv, cache_mla_krope, cache_sb_k, cache_sb_v, cache_df_k, cache_df_v,
           page_table, g_ffn1, w1_gate, w1_up, w1_down, g_mix, w_in, g_q_lat, w_uq, g_kv_lat, w_uk, w_uv,
           g_mla_out, g_sb_out, lam_q1, lam_k1, lam_q2, lam_k2, g_df_sub, w_out, g_ffn2, w2_gate, w2_up, w2_down,
           g_final):
    b, t, d = x_prompt.shape
    db, dt, _ = x_sample.shape
    depth = g_ffn1.shape[0]
    n_pages = page_table.shape[1]
    page = cache_mla_ckv.shape[2]
    assert page == LANES and dt <= DEC_ROWS and d == 1024
    bt, ns = b * t, db * dt
    tq = _pick_tile(t, (512, 256, 128))
    tq_sb = _pick_tile(t, (256, 128))
    tk = _pick_tile(t, (1024, 512, 256, 128))
    npg = _pick_tile(n_pages, (64, 32, 16, 8, 4, 2, 1))
    gp = npg
    sb_recent = 4 if n_pages >= 8 else max(1, n_pages // 2)
    sb_old = n_pages - sb_recent
    sb_npg_old = max((c for c in range(1, 21) if sb_old % c == 0), default=1)

    inv_freq = ROPE_THETA ** (-jnp.arange(0, DF_D, 2, dtype=F32) / DF_D)

    def rope_tables(pos, reps):
        ang = jnp.tile(pos, reps).astype(F32)[:, None] * inv_freq
        return jnp.tile(jnp.cos(ang), (1, LANES // 16)), jnp.tile(jnp.sin(ang), (1, LANES // 16))

    rope_p = rope_tables(jnp.arange(t, dtype=jnp.int32), b)
    rope_s = rope_tables(n_pages * page + jnp.arange(dt, dtype=jnp.int32), db)

    cache_kr_t = jnp.transpose(cache_mla_krope, (0, 1, 3, 2))
    cache_sbk_t = jnp.transpose(cache_sb_k, (0, 1, 3, 4, 2))
    cache_sbv_t = jnp.transpose(cache_sb_v, (0, 1, 3, 4, 2))
    cache_dfk_t = jnp.transpose(cache_df_k, (0, 1, 3, 4, 5, 2))
    cache_dfv_t = jnp.transpose(cache_df_v, (0, 1, 3, 4, 2))

    tri = lambda n: jnp.tril(jnp.ones((n, n), F32)).astype(BF16)
    u_q, u_p, u_new = tri(tq_sb), tri(LANES), tri(DEC_ROWS)

    xp = x_prompt.reshape(bt, d)
    xs = x_sample.reshape(ns, d)
    rows_p = [[] for _ in range(6)]
    rows_s = [[] for _ in range(6)]
    row2 = lambda v: v.reshape(1, -1)
    df_perm = jnp.array([0, 2, 1, 3])

    for l in range(depth):
        lam_init = 0.8 - 0.6 * math.exp(-0.3 * l)
        wi = w_in[l]
        q_df_cols = wi[:, 1696:1952].reshape(d, DF_HEADS, 2 * DF_D)[:, df_perm].reshape(d, 256)
        w_in_r = jnp.concatenate([wi[:, :512], jnp.pad(wi[:, 512:544], ((0, 0), (0, LANES - MLA_ROPE))),
                                  wi[:, 544:1696], q_df_cols, wi[:, 1952:2208]], axis=1).astype(BF16)
        wq = w_uq[l].reshape(256, MLA_HEADS, MLA_NOPE + MLA_ROPE)
        wq_blocks = []
        for hd in range(MLA_HEADS):
            blk = jnp.zeros((256, 256), F32)
            blk = blk.at[:, (hd % 2) * 64:(hd % 2) * 64 + MLA_NOPE].set(wq[:, hd, :MLA_NOPE])
            blk = blk.at[:, LANES:LANES + MLA_ROPE].set(wq[:, hd, MLA_NOPE:])
            wq_blocks.append(blk)
        w_uq_r = jnp.concatenate(wq_blocks, axis=1).astype(BF16)
        w_uk2 = w_uk[l].reshape(256, 384).astype(BF16)
        w_uv2 = w_uv[l].reshape(256, 384).astype(BF16)
        w_ukt = jnp.transpose(w_uk2.reshape(256, 3, LANES), (1, 2, 0))
        wo = w_out[l]
        w_out_r = jnp.concatenate([wo[:768], wo[768:].reshape(DF_HEADS, DF_V, d)[df_perm].reshape(256, d)],
                                  axis=0).astype(BF16)
        lams = (row2(lam_q1[l]), row2(lam_k1[l]), row2(lam_q2[l]), row2(lam_k2[l]))
        g_df = row2(g_df_sub[l])
        g_df2 = jnp.tile(g_df, (1, 2))

        ffn1 = functools.partial(_ffn, g=row2(g_ffn1[l]), wg=w1_gate[l].astype(BF16), wu=w1_up[l].astype(BF16),
                                 wd=w1_down[l].astype(BF16), g_final=row2(g_final), final_norm=False)
        mix_ffn2 = functools.partial(_mix_ffn, g_mla=row2(g_mla_out[l]), g_sb=row2(g_sb_out[l]), w_out=w_out_r,
                                     g=row2(g_ffn2[l]), wg=w2_gate[l].astype(BF16), wu=w2_up[l].astype(BF16),
                                     wd=w2_down[l].astype(BF16), g_final=row2(g_final), final_norm=l == depth - 1)
        inproj = functools.partial(_inproj, g_mix=row2(g_mix[l]), w_in=w_in_r, g_q=row2(g_q_lat[l]), w_uq=w_uq_r,
                                   g_kv=row2(g_kv_lat[l]), w_uk=w_uk2, w_uv=w_uv2)

        xp = ffn1(xp)
        xs = ffn1(xs)
        new_p = inproj(xp, cos=rope_p[0], sin=rope_p[1])
        new_s = inproj(xs, cos=rope_s[0], sin=rope_s[1])

        (_, _, _, _, _, _, qm, km, vm, qsb, ksb_b, vsb_b, qdf, kdf_b, vdf_b) = new_p
        o_mla_p = _mla_prompt(qm, km, vm, b, t, tq, tk)
        o_sb_p = _sb_prompt(qsb, ksb_b, vsb_b, u_q, b, t, tq_sb)
        o_df_p = _df_prompt(qdf, kdf_b, vdf_b, lams, g_df2, b, t, tq, tk, lam_init)

        ckv, krope, ksb, vsb, kdf, vdf, qm, _, _, qsb, _, _, qdf, _, _ = [_sample_rows(a, db, dt) for a in new_s]
        o_mla_s = _mla_dec(page_table, qm, ckv, krope, w_ukt, w_uv2, cache_mla_ckv, cache_kr_t, l, npg, gp, dt)
        o_sb_s = _sb_dec(page_table, qsb, ksb, vsb, u_p, u_new, cache_sbk_t, cache_sbv_t, l, sb_recent,
                         sb_npg_old, dt).reshape(db, SB_HEADS, DEC_ROWS, SB_DIM)
        o_df_s = _df_dec(page_table, qdf, kdf, vdf, lams, g_df, cache_dfk_t, cache_dfv_t, l, npg, gp, dt, lam_init)

        xp = mix_ffn2(xp, o_mla_p, o_sb_p, o_df_p)
        xs = mix_ffn2(xs, _heads_to_rows(o_mla_s, dt), _heads_to_rows(o_sb_s, dt),
                      _heads_to_rows(o_df_s, dt, _DF_HEAD_ORDER))

        for i in range(6):
            rows_p[i].append(new_p[i])
            rows_s[i].append(new_s[i])

    def stack(rows, lead, tail):
        return jnp.stack(rows, axis=0).reshape((depth,) + lead + tail)

    tails = ((256,), (MLA_ROPE,), (SB_HEADS, SB_DIM), (SB_HEADS, SB_DIM), (DF_KV_HEADS, 2, DF_D), (DF_KV_HEADS, DF_V))
    outs = [xp.reshape(b, t, d), xs.reshape(db, dt, d)]
    for i in range(6):
        outs.append(stack(rows_p[i], (b, t), tails[i]))
        outs.append(stack(rows_s[i], (db, dt), tails[i]))
    return tuple(outs)
```
